```python
import math
import jax, jax.numpy as jnp
from jax import lax
import numpy as np

D_MODEL = 2048
BATCH = 4
SEQ = 4096
DEPTH = 1

EPS = 1e-6
MLA_HEADS = 8
QK_NOPE_DIM = 128
QK_ROPE_DIM = 64
V_HEAD_DIM = 128
QK_HEAD_DIM = QK_NOPE_DIM + QK_ROPE_DIM
Q_LORA_RANK = 512
KV_LORA_RANK = 512
MLA_WIDTH = MLA_HEADS * V_HEAD_DIM
CONV_GROUPS = 8
CONV_GROUP_DIM = 128
CONV_WIDTH = CONV_GROUPS * CONV_GROUP_DIM
MIX_WIDTH = MLA_WIDTH + CONV_WIDTH
CONV_KERNEL = 3
ROPE_THETA = 10000.0
Q_BLOCK = 128
IN_SPLITS = (Q_LORA_RANK, KV_LORA_RANK, QK_ROPE_DIM, CONV_WIDTH, CONV_WIDTH, CONV_WIDTH)
IN_WIDTH = sum(IN_SPLITS)
D_FF = 5632

kernel_name = "hymba_mla_shortconv_convffn_encoder_block"


def rmsnorm(x, g):
    xf = x.astype(jnp.float32)
    y = xf * lax.rsqrt(jnp.mean(xf * xf, axis=-1, keepdims=True) + EPS)
    return (y * g.astype(jnp.float32)).astype(x.dtype)


def conv3_centred(h, w):
    hp = jnp.pad(h, ((0, 0), (1, 1), (0, 0)))
    return hp[:, :-2] * w[0] + hp[:, 1:-1] * w[1] + hp[:, 2:] * w[2]


def rope_tables(seq, dim, dtype):
    pos = jnp.arange(seq, dtype=jnp.float32)
    inv_freq = 1.0 / (ROPE_THETA ** (jnp.arange(0, dim, 2, dtype=jnp.float32) / dim))
    ang = pos[:, None] * inv_freq[None, :]
    return jnp.cos(ang).astype(dtype), jnp.sin(ang).astype(dtype)


def apply_rope(x, cos, sin):
    half = x.shape[-1] // 2
    x1, x2 = x[..., :half], x[..., half:]
    c = cos[None, :, None, :]
    s = sin[None, :, None, :]
    return jnp.concatenate([x1 * c - x2 * s, x1 * s + x2 * c], axis=-1)


def dense_bidirectional_attention(q, k, v):
    b, s, h, dqk = q.shape
    dv = v.shape[-1]
    nb = s // Q_BLOCK
    scale = 1.0 / math.sqrt(dqk)
    qb = q.reshape(b, nb, Q_BLOCK, h, dqk).transpose(1, 0, 2, 3, 4)

    def one_block(q_blk):
        scores = jnp.einsum('bqhd,bkhd->bhqk', q_blk, k).astype(jnp.float32) * scale
        probs = jax.nn.softmax(scores, axis=-1).astype(v.dtype)
        return jnp.einsum('bhqk,bkhd->bqhd', probs, v)

    out = lax.map(one_block, qb)
    return out.transpose(1, 0, 2, 3, 4).reshape(b, s, h, dv)


def setup_inputs(seed: int = 0) -> dict:
    key = jax.random.key(seed)
    ks = jax.random.split(key, 20)
    f32 = jnp.float32

    def w(k, shape, fan_in):
        return jax.random.normal(k, shape, f32) * (fan_in ** -0.5)

    def gain(k, shape):
        return 1.0 + 0.02 * jax.random.normal(k, shape, f32)

    L = DEPTH
    return {
        "x": jax.random.normal(ks[0], (BATCH, SEQ, D_MODEL), f32),
        "attn_norm_g": gain(ks[1], (L, D_MODEL)),
        "w_in": w(ks[2], (L, D_MODEL, IN_WIDTH), D_MODEL),
        "q_a_norm_g": gain(ks[3], (L, Q_LORA_RANK)),
        "kv_a_norm_g": gain(ks[4], (L, KV_LORA_RANK)),
        "w_q_b": w(ks[5], (L, Q_LORA_RANK, MLA_HEADS * QK_HEAD_DIM), Q_LORA_RANK),
        "w_kv_b": w(ks[6], (L, KV_LORA_RANK, MLA_HEADS * (QK_NOPE_DIM + V_HEAD_DIM)), KV_LORA_RANK),
        "sc_conv_w": w(ks[7], (L, CONV_KERNEL, CONV_WIDTH), CONV_KERNEL),
        "out_norm_attn_g": gain(ks[8], (L, MLA_WIDTH)),
        "out_norm_conv_g": gain(ks[9], (L, CONV_WIDTH)),
        "w_o": w(ks[10], (L, MIX_WIDTH, D_MODEL), MIX_WIDTH),
        "ffn_norm_g": gain(ks[11], (L, D_MODEL)),
        "w_ffn_up": w(ks[12], (L, D_MODEL, 2 * D_FF), D_MODEL),
        "ffn_conv_w": w(ks[13], (L, CONV_KERNEL, 2 * D_FF), CONV_KERNEL),
        "ffn_conv_b": 0.01 * jax.random.normal(ks[14], (L, 2 * D_FF), f32),
        "w_ffn_down": w(ks[15], (L, D_FF, D_MODEL), D_FF),
        "final_norm_g": gain(ks[16], (D_MODEL,)),
    }


def reference(x, attn_norm_g, w_in, q_a_norm_g, kv_a_norm_g, w_q_b, w_kv_b,
              sc_conv_w, out_norm_attn_g, out_norm_conv_g, w_o, ffn_norm_g,
              w_ffn_up, ffn_conv_w, ffn_conv_b, w_ffn_down, final_norm_g):
    b, s, _ = x.shape
    cos, sin = rope_tables(s, QK_ROPE_DIM, x.dtype)
    split_points = list(np.cumsum(IN_SPLITS)[:-1])

    for l in range(DEPTH):
        h = rmsnorm(x, attn_norm_g[l])
        z = h @ w_in[l]
        c_q, c_kv, k_rope, gate_b, gate_c, sc_h = jnp.split(z, split_points, axis=-1)

        q = (rmsnorm(c_q, q_a_norm_g[l]) @ w_q_b[l]).reshape(b, s, MLA_HEADS, QK_HEAD_DIM)
        q = jnp.concatenate([q[..., :QK_NOPE_DIM], apply_rope(q[..., QK_NOPE_DIM:], cos, sin)], axis=-1)
        kv = (rmsnorm(c_kv, kv_a_norm_g[l]) @ w_kv_b[l]).reshape(b, s, MLA_HEADS, QK_NOPE_DIM + V_HEAD_DIM)
        k_nope, v = kv[..., :QK_NOPE_DIM], kv[..., QK_NOPE_DIM:]
        k_pe = apply_rope(k_rope[:, :, None, :], cos, sin)
        k = jnp.concatenate([k_nope, jnp.broadcast_to(k_pe, (b, s, MLA_HEADS, QK_ROPE_DIM))], axis=-1)
        attn = dense_bidirectional_attention(q, k, v).reshape(b, s, MLA_WIDTH)

        y_conv = gate_b * conv3_centred(gate_c * sc_h, sc_conv_w[l])

        merged = jnp.concatenate([rmsnorm(attn, out_norm_attn_g[l]),
                                  rmsnorm(y_conv, out_norm_conv_g[l])], axis=-1)
        x = x + merged @ w_o[l]

        h = rmsnorm(x, ffn_norm_g[l])
        u = conv3_centred(h @ w_ffn_up[l], ffn_conv_w[l]) + ffn_conv_b[l]
        g, val = u[..., :D_FF], u[..., D_FF:]
        x = x + (jax.nn.silu(g) * val) @ w_ffn_down[l]

    return rmsnorm(x, final_norm_g)
```

```python
import functools
import math

import jax
import jax.numpy as jnp
from jax import lax
from jax.experimental import pallas as pl
from jax.experimental.pallas import tpu as pltpu

F32 = jnp.float32
BF16 = jnp.bfloat16

EPS = 1e-6
MLA_HEADS = 8
QK_NOPE_DIM = 128
QK_ROPE_DIM = 64
V_HEAD_DIM = 128
QK_HEAD_DIM = QK_NOPE_DIM + QK_ROPE_DIM
ROPE_THETA = 10000.0

LANES = 128
SUBLANES = 8
BF16_ROWS = 16
VMEM_LIMIT_BYTES = 56 * 1024 * 1024

QK_PAD_DIM = QK_NOPE_DIM + LANES
HALF_ROPE = QK_ROPE_DIM // 2


def _rms(xf, g):
    ms = jnp.mean(xf * xf, axis=-1, keepdims=True)
    return xf * lax.rsqrt(ms + EPS) * g


def _rope(z, cos_t, sin_t):
    return z * cos_t + pltpu.roll(z, LANES // 2, 1) * sin_t


def _compiler_params(semantics):
    return pltpu.CompilerParams(dimension_semantics=semantics,
                                vmem_limit_bytes=VMEM_LIMIT_BYTES)


def _latent_kernel(x_ref, g_ref, wlat_ref, gq_ref, gkv_ref, wqb_ref, wkvb_ref,
                   cos_ref, sin_ref, q_ref, k_ref, v_ref, *, q_rank, kv_rank, scale):
    h = _rms(x_ref[...], g_ref[...]).astype(BF16)
    z = jnp.dot(h, wlat_ref[...], preferred_element_type=F32)
    cq = _rms(z[:, :q_rank], gq_ref[...]).astype(BF16)
    ckv = _rms(z[:, q_rank:q_rank + kv_rank], gkv_ref[...]).astype(BF16)
    cos_t = cos_ref[...]
    sin_t = sin_ref[...]
    k_pe = _rope(z[:, q_rank + kv_rank:], cos_t, sin_t).astype(BF16)
    q = jnp.dot(cq, wqb_ref[...], preferred_element_type=F32)
    kv = jnp.dot(ckv, wkvb_ref[...], preferred_element_type=F32)
    for hh in range(MLA_HEADS):
        lo = hh * QK_PAD_DIM
        mid = lo + QK_NOPE_DIM
        hi = lo + QK_PAD_DIM
        q_ref[0, hh, :, 0:QK_NOPE_DIM] = (q[:, lo:mid] * scale).astype(BF16)
        q_ref[0, hh, :, QK_NOPE_DIM:QK_PAD_DIM] = (
            _rope(q[:, mid:hi], cos_t, sin_t) * scale).astype(BF16)
        k_ref[0, hh, :, 0:QK_NOPE_DIM] = kv[:, lo:mid].astype(BF16)
        k_ref[0, hh, :, QK_NOPE_DIM:QK_PAD_DIM] = k_pe
        v_ref[0, hh] = kv[:, mid:hi].astype(BF16)


def _latent_call(x2, g, wlat, gq, gkv, wqb, wkvb, cos_t, sin_t, *, batch, seq, tm):
    n, d = x2.shape
    q_rank = gq.shape[1]
    kv_rank = gkv.shape[1]
    tiles_per_seq = seq // tm
    const = lambda i: (0, 0)
    head_map = lambda i: (i // tiles_per_seq, 0, i % tiles_per_seq, 0)
    kern = functools.partial(_latent_kernel, q_rank=q_rank, kv_rank=kv_rank,
                             scale=1.0 / math.sqrt(QK_HEAD_DIM))
    return pl.pallas_call(
        kern,
        grid=(n // tm,),
        in_specs=[
            pl.BlockSpec((tm, d), lambda i: (i, 0)),
            pl.BlockSpec(g.shape, const),
            pl.BlockSpec(wlat.shape, const),
            pl.BlockSpec(gq.shape, const),
            pl.BlockSpec(gkv.shape, const),
            pl.BlockSpec(wqb.shape, const),
            pl.BlockSpec(wkvb.shape, const),
            pl.BlockSpec((tm, LANES), lambda i: (i % tiles_per_seq, 0)),
            pl.BlockSpec((tm, LANES), lambda i: (i % tiles_per_seq, 0)),
        ],
        out_specs=[
            pl.BlockSpec((1, MLA_HEADS, tm, QK_PAD_DIM), head_map),
            pl.BlockSpec((1, MLA_HEADS, tm, QK_PAD_DIM), head_map),
            pl.BlockSpec((1, MLA_HEADS, tm, V_HEAD_DIM), head_map),
        ],
        out_shape=[
            jax.ShapeDtypeStruct((batch, MLA_HEADS, seq, QK_PAD_DIM), BF16),
            jax.ShapeDtypeStruct((batch, MLA_HEADS, seq, QK_PAD_DIM), BF16),
            jax.ShapeDtypeStruct((batch, MLA_HEADS, seq, V_HEAD_DIM), BF16),
        ],
        compiler_params=_compiler_params(("arbitrary",)),
        name="latent_qkv",
    )(x2, g, wlat, gq, gkv, wqb, wkvb, cos_t, sin_t)


def _gate_kernel(x_ref, g_ref, wb_ref, wc_ref, wh_ref, gb_ref, ch_ref):
    h = _rms(x_ref[...], g_ref[...]).astype(BF16)
    gb_ref[...] = jnp.dot(h, wb_ref[...], preferred_element_type=F32).astype(BF16)
    zc = jnp.dot(h, wc_ref[...], preferred_element_type=F32)
    zh = jnp.dot(h, wh_ref[...], preferred_element_type=F32)
    ch_ref[...] = (zc * zh).astype(BF16)


def _gate_call(x2, g, wbch, *, tm):
    n, d = x2.shape
    cw = wbch.shape[1] // 3
    return pl.pallas_call(
        _gate_kernel,
        grid=(n // tm,),
        in_specs=[
            pl.BlockSpec((tm, d), lambda i: (i, 0)),
            pl.BlockSpec(g.shape, lambda i: (0, 0)),
            pl.BlockSpec((d, cw), lambda i: (0, 0)),
            pl.BlockSpec((d, cw), lambda i: (0, 1)),
            pl.BlockSpec((d, cw), lambda i: (0, 2)),
        ],
        out_specs=[
            pl.BlockSpec((tm, cw), lambda i: (i, 0)),
            pl.BlockSpec((tm, cw), lambda i: (i, 0)),
        ],
        out_shape=[
            jax.ShapeDtypeStruct((n, cw), BF16),
            jax.ShapeDtypeStruct((n, cw), BF16),
        ],
        compiler_params=_compiler_params(("arbitrary",)),
        name="gate_proj",
    )(x2, g, wbch, wbch, wbch)


def _attn_kernel(q_ref, k_ref, v_ref, o_ref, *, tq, tk):
    seq = k_ref.shape[2]

    def q_tile(qi, carry):
        row0 = pl.multiple_of(qi * tq, tq)
        q = q_ref[0, 0, pl.ds(row0, tq), :]
        m = jnp.full((tq, 1), -jnp.inf, F32)
        l = jnp.zeros((tq, 1), F32)
        acc = jnp.zeros((tq, V_HEAD_DIM), F32)
        for kc in range(seq // tk):
            k = k_ref[0, 0, kc * tk:(kc + 1) * tk, :]
            v = v_ref[0, 0, kc * tk:(kc + 1) * tk, :]
            s = lax.dot_general(q, k, (((1,), (1,)), ((), ())),
                                preferred_element_type=F32)
            m_new = jnp.maximum(m, jnp.max(s, axis=-1, keepdims=True))
            p = jnp.exp(s - m_new)
            alpha = jnp.exp(m - m_new)
            l = alpha * l + jnp.sum(p, axis=-1, keepdims=True)
            acc = alpha * acc + jnp.dot(p.astype(BF16), v, preferred_element_type=F32)
            m = m_new
        o_ref[0, pl.ds(row0, tq), :] = (acc / l).astype(o_ref.dtype)
        return carry

    lax.fori_loop(0, seq // tq, q_tile, 0)


def _attn_call(q, k, v, *, tq, tk):
    batch, heads, seq, _ = q.shape
    kern = functools.partial(_attn_kernel, tq=tq, tk=tk)
    return pl.pallas_call(
        kern,
        grid=(batch, heads),
        in_specs=[
            pl.BlockSpec((1, 1, seq, QK_PAD_DIM), lambda b, h: (b, h, 0, 0)),
            pl.BlockSpec((1, 1, seq, QK_PAD_DIM), lambda b, h: (b, h, 0, 0)),
            pl.BlockSpec((1, 1, seq, V_HEAD_DIM), lambda b, h: (b, h, 0, 0)),
        ],
        out_specs=pl.BlockSpec((1, seq, V_HEAD_DIM), lambda b, h: (b, 0, h)),
        out_shape=jax.ShapeDtypeStruct((batch, seq, heads * V_HEAD_DIM), BF16),
        compiler_params=_compiler_params(("arbitrary", "arbitrary")),
        name="mla_attention",
    )(q, k, v)


def _shift_rows(a, prev_row, next_row):
    tm = a.shape[0]
    row = lax.broadcasted_iota(jnp.int32, a.shape, 0)
    a_m1 = jnp.where(row == 0, prev_row, pltpu.roll(a, 1, 0))
    a_p1 = jnp.where(row == tm - 1, next_row, pltpu.roll(a, tm - 1, 0))
    return a_m1, a_p1


def _outproj_kernel(attn_ref, gb_ref, ch_ref, chp_ref, chn_ref, x_ref, wo_ref, cw_ref,
                    ga_ref, gc_ref, o_ref, merged_ref, *, tiles_per_seq):
    i = pl.program_id(0)
    aw = attn_ref.shape[1]
    merged_ref[:, :aw] = _rms(attn_ref[...].astype(F32), ga_ref[...]).astype(BF16)

    not_first = (i % tiles_per_seq != 0).astype(F32)
    not_last = (i % tiles_per_seq != tiles_per_seq - 1).astype(F32)
    ch = ch_ref[...].astype(F32)
    prev_row = chp_ref[SUBLANES - 1:SUBLANES, :].astype(F32) * not_first
    next_row = chn_ref[0:1, :].astype(F32) * not_last
    ch_m1, ch_p1 = _shift_rows(ch, prev_row, next_row)
    cw = cw_ref[...]
    y = gb_ref[...].astype(F32) * (ch_m1 * cw[0:1, :] + ch * cw[1:2, :] + ch_p1 * cw[2:3, :])
    merged_ref[:, aw:] = _rms(y, gc_ref[...]).astype(BF16)

    o_ref[...] = x_ref[...] + jnp.dot(merged_ref[...], wo_ref[...],
                                      preferred_element_type=F32)


def _outproj_call(attn, gb, ch, x2, wo, cw, ga, gc, *, seq, tm):
    n, d = x2.shape
    aw = attn.shape[1]
    cwid = gb.shape[1]
    tiles_per_seq = seq // tm
    blk8 = tm // SUBLANES
    last8 = n // SUBLANES - 1
    const = lambda i: (0, 0)
    kern = functools.partial(_outproj_kernel, tiles_per_seq=tiles_per_seq)
    return pl.pallas_call(
        kern,
        grid=(n // tm,),
        in_specs=[
            pl.BlockSpec((tm, aw), lambda i: (i, 0)),
            pl.BlockSpec((tm, cwid), lambda i: (i, 0)),
            pl.BlockSpec((tm, cwid), lambda i: (i, 0)),
            pl.BlockSpec((SUBLANES, cwid), lambda i: (jnp.maximum(i * blk8 - 1, 0), 0)),
            pl.BlockSpec((SUBLANES, cwid), lambda i: (jnp.minimum((i + 1) * blk8, last8), 0)),
            pl.BlockSpec((tm, d), lambda i: (i, 0)),
            pl.BlockSpec(wo.shape, const),
            pl.BlockSpec(cw.shape, const),
            pl.BlockSpec(ga.shape, const),
            pl.BlockSpec(gc.shape, const),
        ],
        out_specs=pl.BlockSpec((tm, d), lambda i: (i, 0)),
        out_shape=jax.ShapeDtypeStruct((n, d), F32),
        scratch_shapes=[pltpu.VMEM((tm, aw + cwid), BF16)],
        compiler_params=_compiler_params(("arbitrary",)),
        name="out_proj",
    )(attn, gb, ch, ch, ch, x2, wo, cw, ga, gc)


def _ffn_kernel(x_ref, xp_ref, xn_ref, g_ref, wg_ref, wv_ref, cwg_ref, cwv_ref,
                bg_ref, bv_ref, wd_ref, gf_ref, o_ref, h_ref, acc_ref, *, tiles_per_seq):
    i = pl.program_id(0)
    j = pl.program_id(1)
    tm = x_ref.shape[0]

    @pl.when(j == 0)
    def _():
        g = g_ref[...]
        h_ref[0:tm, :] = _rms(x_ref[...], g).astype(BF16)
        not_first = (i % tiles_per_seq != 0).astype(F32)
        not_last = (i % tiles_per_seq != tiles_per_seq - 1).astype(F32)
        halo = jnp.concatenate([_rms(xp_ref[...], g) * not_first,
                                _rms(xn_ref[...], g) * not_last], axis=0)
        h_ref[tm:tm + BF16_ROWS, :] = halo.astype(BF16)
        acc_ref[...] = jnp.zeros_like(acc_ref)

    h = h_ref[...]

    def conv_branch(w_ref, cw_ref, b_ref):
        a = jnp.dot(h, w_ref[...], preferred_element_type=F32)
        a_m1, a_p1 = _shift_rows(a[0:tm], a[tm + SUBLANES - 1:tm + SUBLANES],
                                 a[tm + SUBLANES:tm + SUBLANES + 1])
        cw = cw_ref[...]
        return a_m1 * cw[0:1, :] + a[0:tm] * cw[1:2, :] + a_p1 * cw[2:3, :] + b_ref[...]

    ug = conv_branch(wg_ref, cwg_ref, bg_ref)
    uv = conv_branch(wv_ref, cwv_ref, bv_ref)
    act = (ug * (1.0 / (1.0 + jnp.exp(-ug))) * uv).astype(BF16)
    acc_ref[...] += jnp.dot(act, wd_ref[...], preferred_element_type=F32)

    @pl.when(j == pl.num_programs(1) - 1)
    def _():
        o_ref[...] = _rms(x_ref[...] + acc_ref[...], gf_ref[...])


def _ffn_call(x1, g, wup, cw, b, wd, gf, *, seq, tm, tn):
    n, d = x1.shape
    dff = wd.shape[0]
    nj = dff // tn
    tiles_per_seq = seq // tm
    blk8 = tm // SUBLANES
    last8 = n // SUBLANES - 1
    const = lambda i, j: (0, 0)
    kern = functools.partial(_ffn_kernel, tiles_per_seq=tiles_per_seq)
    return pl.pallas_call(
        kern,
        grid=(n // tm, nj),
        in_specs=[
            pl.BlockSpec((tm, d), lambda i, j: (i, 0)),
            pl.BlockSpec((SUBLANES, d), lambda i, j: (jnp.maximum(i * blk8 - 1, 0), 0)),
            pl.BlockSpec((SUBLANES, d), lambda i, j: (jnp.minimum((i + 1) * blk8, last8), 0)),
            pl.BlockSpec(g.shape, const),
            pl.BlockSpec((d, tn), lambda i, j: (0, j)),
            pl.BlockSpec((d, tn), lambda i, j: (0, j + nj)),
            pl.BlockSpec((3, tn), lambda i, j: (0, j)),
            pl.BlockSpec((3, tn), lambda i, j: (0, j + nj)),
            pl.BlockSpec((1, tn), lambda i, j: (0, j)),
            pl.BlockSpec((1, tn), lambda i, j: (0, j + nj)),
            pl.BlockSpec((tn, d), lambda i, j: (j, 0)),
            pl.BlockSpec(gf.shape, const),
        ],
        out_specs=pl.BlockSpec((tm, d), lambda i, j: (i, 0)),
        out_shape=jax.ShapeDtypeStruct((n, d), F32),
        scratch_shapes=[pltpu.VMEM((tm + BF16_ROWS, d), BF16),
                        pltpu.VMEM((tm, d), F32)],
        compiler_params=_compiler_params(("arbitrary", "arbitrary")),
        name="conv_ffn",
    )(x1, x1, x1, g, wup, wup, cw, cw, b, b, wd, gf)


def _rope_tables(seq):
    pos = jnp.arange(seq, dtype=F32)
    inv_freq = 1.0 / (ROPE_THETA ** (jnp.arange(0, QK_ROPE_DIM, 2, dtype=F32) / QK_ROPE_DIM))
    ang = pos[:, None] * inv_freq[None, :]
    c = jnp.cos(ang)
    s = jnp.sin(ang)
    return (jnp.concatenate([c, c, c, c], axis=-1),
            jnp.concatenate([-s, -s, s, s], axis=-1))


def _pad_rope_cols(w):
    z = jnp.zeros(w.shape[:-1] + (HALF_ROPE,), w.dtype)
    return jnp.concatenate([w[..., :HALF_ROPE], z, w[..., HALF_ROPE:], z], axis=-1)


def _layer(x2, p, cos_t, sin_t, *, batch, seq):
    d = x2.shape[1]
    q_rank = p["q_a_norm_g"].shape[0]
    kv_rank = p["kv_a_norm_g"].shape[0]
    lat = q_rank + kv_rank
    w_in = p["w_in"]
    wlat = jnp.concatenate(
        [w_in[:, :lat], _pad_rope_cols(w_in[:, lat:lat + QK_ROPE_DIM])], axis=1).astype(BF16)
    wbch = w_in[:, lat + QK_ROPE_DIM:].astype(BF16)
    wq = p["w_q_b"].reshape(q_rank, MLA_HEADS, QK_HEAD_DIM)
    wqb = jnp.concatenate([wq[..., :QK_NOPE_DIM], _pad_rope_cols(wq[..., QK_NOPE_DIM:])],
                          axis=-1).reshape(q_rank, MLA_HEADS * QK_PAD_DIM).astype(BF16)
    wkvb = p["w_kv_b"].astype(BF16)
    row = lambda v: v.reshape(1, -1)

    q, k, v = _latent_call(x2, row(p["attn_norm_g"]), wlat, row(p["q_a_norm_g"]),
                           row(p["kv_a_norm_g"]), wqb, wkvb, cos_t, sin_t,
                           batch=batch, seq=seq, tm=512)
    gb, ch = _gate_call(x2, row(p["attn_norm_g"]), wbch, tm=512)
    attn = _attn_call(q, k, v, tq=512, tk=1024).reshape(batch * seq, -1)
    x1 = _outproj_call(attn, gb, ch, x2, p["w_o"].astype(BF16), p["sc_conv_w"],
                       row(p["out_norm_attn_g"]), row(p["out_norm_conv_g"]), seq=seq, tm=512)
    return x1


def kernel(x, attn_norm_g, w_in, q_a_norm_g, kv_a_norm_g, w_q_b, w_kv_b, sc_conv_w,
           out_norm_attn_g, out_norm_conv_g, w_o, ffn_norm_g, w_ffn_up, ffn_conv_w,
           ffn_conv_b, w_ffn_down, final_norm_g):
    batch, seq, d = x.shape
    depth = w_in.shape[0]
    assert depth == 1, "the final RMSNorm is fused into the (single) ConvFFN layer"
    cos_t, sin_t = _rope_tables(seq)
    x2 = x.reshape(batch * seq, d)
    p = dict(attn_norm_g=attn_norm_g[0], w_in=w_in[0], q_a_norm_g=q_a_norm_g[0],
             kv_a_norm_g=kv_a_norm_g[0], w_q_b=w_q_b[0], w_kv_b=w_kv_b[0],
             sc_conv_w=sc_conv_w[0], out_norm_attn_g=out_norm_attn_g[0],
             out_norm_conv_g=out_norm_conv_g[0], w_o=w_o[0])
    x1 = _layer(x2, p, cos_t, sin_t, batch=batch, seq=seq)
    out = _ffn_call(x1, ffn_norm_g[0].reshape(1, -1), w_ffn_up[0].astype(BF16), ffn_conv_w[0],
                    ffn_conv_b[0].reshape(1, -1), w_ffn_down[0].astype(BF16),
                    final_norm_g.reshape(1, -1), seq=seq, tm=512, tn=512)
    return out.reshape(batch, seq, d)
```

```python
import functools
import math

import jax
import jax.numpy as jnp
from jax import lax
from jax.experimental import pallas as pl
from jax.experimental.pallas import tpu as pltpu

F32 = jnp.float32
BF16 = jnp.bfloat16

EPS = 1e-6
MLA_HEADS = 8
QK_NOPE_DIM = 128
QK_ROPE_DIM = 64
V_HEAD_DIM = 128
QK_HEAD_DIM = QK_NOPE_DIM + QK_ROPE_DIM
ROPE_THETA = 10000.0

LANES = 128
SUBLANES = 8
BF16_ROWS = 16
MXU_COLS = 256
VMEM_LIMIT_BYTES = 56 * 1024 * 1024

QK_PAD_DIM = QK_NOPE_DIM + LANES
HALF_ROPE = QK_ROPE_DIM // 2


def _rms(xf, g):
    ms = jnp.mean(xf * xf, axis=-1, keepdims=True)
    return xf * lax.rsqrt(ms + EPS) * g


def _rope(z, cos_t, sin_t):
    return z * cos_t + pltpu.roll(z, LANES // 2, 1) * sin_t


def _compiler_params(semantics):
    return pltpu.CompilerParams(dimension_semantics=semantics,
                                vmem_limit_bytes=VMEM_LIMIT_BYTES)


def _latent_kernel(x_ref, g_ref, wlat_ref, gq_ref, gkv_ref, wqb_ref, wkvb_ref,
                   cos_ref, sin_ref, q_ref, k_ref, v_ref, *, q_rank, kv_rank, scale):
    h = _rms(x_ref[...], g_ref[...]).astype(BF16)
    z = jnp.dot(h, wlat_ref[...], preferred_element_type=F32)
    cq = _rms(z[:, :q_rank], gq_ref[...]).astype(BF16)
    ckv = _rms(z[:, q_rank:q_rank + kv_rank], gkv_ref[...]).astype(BF16)
    cos_t = cos_ref[...]
    sin_t = sin_ref[...]
    k_pe = _rope(z[:, q_rank + kv_rank:], cos_t, sin_t).astype(BF16)
    q = jnp.dot(cq, wqb_ref[...], preferred_element_type=F32)
    kv = jnp.dot(ckv, wkvb_ref[...], preferred_element_type=F32)
    for hh in range(MLA_HEADS):
        lo = hh * QK_PAD_DIM
        mid = lo + QK_NOPE_DIM
        hi = lo + QK_PAD_DIM
        q_ref[0, hh, :, 0:QK_NOPE_DIM] = (q[:, lo:mid] * scale).astype(BF16)
        q_ref[0, hh, :, QK_NOPE_DIM:QK_PAD_DIM] = (
            _rope(q[:, mid:hi], cos_t, sin_t) * scale).astype(BF16)
        k_ref[0, hh, :, 0:QK_NOPE_DIM] = kv[:, lo:mid].astype(BF16)
        k_ref[0, hh, :, QK_NOPE_DIM:QK_PAD_DIM] = k_pe
        v_ref[0, hh] = kv[:, mid:hi].astype(BF16)


def _latent_call(x2, g, wlat, gq, gkv, wqb, wkvb, cos_t, sin_t, *, batch, seq, tm):
    n, d = x2.shape
    q_rank = gq.shape[1]
    kv_rank = gkv.shape[1]
    tiles_per_seq = seq // tm
    const = lambda i: (0, 0)
    head_map = lambda i: (i // tiles_per_seq, 0, i % tiles_per_seq, 0)
    kern = functools.partial(_latent_kernel, q_rank=q_rank, kv_rank=kv_rank,
                             scale=1.0 / math.sqrt(QK_HEAD_DIM))
    return pl.pallas_call(
        kern,
        grid=(n // tm,),
        in_specs=[
            pl.BlockSpec((tm, d), lambda i: (i, 0)),
            pl.BlockSpec(g.shape, const),
            pl.BlockSpec(wlat.shape, const),
            pl.BlockSpec(gq.shape, const),
            pl.BlockSpec(gkv.shape, const),
            pl.BlockSpec(wqb.shape, const),
            pl.BlockSpec(wkvb.shape, const),
            pl.BlockSpec((tm, LANES), lambda i: (i % tiles_per_seq, 0)),
            pl.BlockSpec((tm, LANES), lambda i: (i % tiles_per_seq, 0)),
        ],
        out_specs=[
            pl.BlockSpec((1, MLA_HEADS, tm, QK_PAD_DIM), head_map),
            pl.BlockSpec((1, MLA_HEADS, tm, QK_PAD_DIM), head_map),
            pl.BlockSpec((1, MLA_HEADS, tm, V_HEAD_DIM), head_map),
        ],
        out_shape=[
            jax.ShapeDtypeStruct((batch, MLA_HEADS, seq, QK_PAD_DIM), BF16),
            jax.ShapeDtypeStruct((batch, MLA_HEADS, seq, QK_PAD_DIM), BF16),
            jax.ShapeDtypeStruct((batch, MLA_HEADS, seq, V_HEAD_DIM), BF16),
        ],
        compiler_params=_compiler_params(("arbitrary",)),
        name="latent_qkv",
    )(x2, g, wlat, gq, gkv, wqb, wkvb, cos_t, sin_t)


def _gate_kernel(x_ref, g_ref, wb_ref, wc_ref, wh_ref, gb_ref, ch_ref):
    h = _rms(x_ref[...], g_ref[...]).astype(BF16)
    gb_ref[...] = jnp.dot(h, wb_ref[...], preferred_element_type=F32).astype(BF16)
    zc = jnp.dot(h, wc_ref[...], preferred_element_type=F32)
    zh = jnp.dot(h, wh_ref[...], preferred_element_type=F32)
    ch_ref[...] = (zc * zh).astype(BF16)


def _gate_call(x2, g, wbch, *, tm):
    n, d = x2.shape
    cw = wbch.shape[1] // 3
    return pl.pallas_call(
        _gate_kernel,
        grid=(n // tm,),
        in_specs=[
            pl.BlockSpec((tm, d), lambda i: (i, 0)),
            pl.BlockSpec(g.shape, lambda i: (0, 0)),
            pl.BlockSpec((d, cw), lambda i: (0, 0)),
            pl.BlockSpec((d, cw), lambda i: (0, 1)),
            pl.BlockSpec((d, cw), lambda i: (0, 2)),
        ],
        out_specs=[
            pl.BlockSpec((tm, cw), lambda i: (i, 0)),
            pl.BlockSpec((tm, cw), lambda i: (i, 0)),
        ],
        out_shape=[
            jax.ShapeDtypeStruct((n, cw), BF16),
            jax.ShapeDtypeStruct((n, cw), BF16),
        ],
        compiler_params=_compiler_params(("arbitrary",)),
        name="gate_proj",
    )(x2, g, wbch, wbch, wbch)


def _attn_kernel(q_ref, k_ref, v_ref, o_ref, *, tq, tk):
    seq = k_ref.shape[2]

    def q_tile(qi, carry):
        row0 = pl.multiple_of(qi * tq, tq)
        q = q_ref[0, 0, pl.ds(row0, tq), :]
        m = jnp.full((tq, 1), -jnp.inf, F32)
        l = jnp.zeros((tq, 1), F32)
        acc = jnp.zeros((tq, V_HEAD_DIM), F32)
        for kc in range(seq // tk):
            k = k_ref[0, 0, kc * tk:(kc + 1) * tk, :]
            v = v_ref[0, 0, kc * tk:(kc + 1) * tk, :]
            s = lax.dot_general(q, k, (((1,), (1,)), ((), ())),
                                preferred_element_type=F32)
            m_new = jnp.maximum(m, jnp.max(s, axis=-1, keepdims=True))
            p = jnp.exp(s - m_new)
            alpha = jnp.exp(m - m_new)
            l = alpha * l + jnp.sum(p, axis=-1, keepdims=True)
            acc = alpha * acc + jnp.dot(p.astype(BF16), v, preferred_element_type=F32)
            m = m_new
        o_ref[0, pl.ds(row0, tq), :] = (acc / l).astype(o_ref.dtype)
        return carry

    lax.fori_loop(0, seq // tq, q_tile, 0)


def _attn_call(q, k, v, *, tq, tk):
    batch, heads, seq, _ = q.shape
    kern = functools.partial(_attn_kernel, tq=tq, tk=tk)
    return pl.pallas_call(
        kern,
        grid=(batch, heads),
        in_specs=[
            pl.BlockSpec((1, 1, seq, QK_PAD_DIM), lambda b, h: (b, h, 0, 0)),
            pl.BlockSpec((1, 1, seq, QK_PAD_DIM), lambda b, h: (b, h, 0, 0)),
            pl.BlockSpec((1, 1, seq, V_HEAD_DIM), lambda b, h: (b, h, 0, 0)),
        ],
        out_specs=pl.BlockSpec((1, seq, V_HEAD_DIM), lambda b, h: (b, 0, h)),
        out_shape=jax.ShapeDtypeStruct((batch, seq, heads * V_HEAD_DIM), BF16),
        compiler_params=_compiler_params(("arbitrary", "arbitrary")),
        name="mla_attention",
    )(q, k, v)


def _shift_rows(a, prev_row, next_row):
    tm = a.shape[0]
    row = lax.broadcasted_iota(jnp.int32, a.shape, 0)
    a_m1 = jnp.where(row == 0, prev_row, pltpu.roll(a, 1, 0))
    a_p1 = jnp.where(row == tm - 1, next_row, pltpu.roll(a, tm - 1, 0))
    return a_m1, a_p1


def _outproj_kernel(attn_ref, gb_ref, ch_ref, chp_ref, chn_ref, x_ref, wo_ref, cw_ref,
                    ga_ref, gc_ref, o_ref, merged_ref, *, tiles_per_seq):
    i = pl.program_id(0)
    aw = attn_ref.shape[1]
    merged_ref[:, :aw] = _rms(attn_ref[...].astype(F32), ga_ref[...]).astype(BF16)

    not_first = (i % tiles_per_seq != 0).astype(F32)
    not_last = (i % tiles_per_seq != tiles_per_seq - 1).astype(F32)
    ch = ch_ref[...].astype(F32)
    prev_row = chp_ref[SUBLANES - 1:SUBLANES, :].astype(F32) * not_first
    next_row = chn_ref[0:1, :].astype(F32) * not_last
    ch_m1, ch_p1 = _shift_rows(ch, prev_row, next_row)
    cw = cw_ref[...]
    y = gb_ref[...].astype(F32) * (ch_m1 * cw[0:1, :] + ch * cw[1:2, :] + ch_p1 * cw[2:3, :])
    merged_ref[:, aw:] = _rms(y, gc_ref[...]).astype(BF16)

    o_ref[...] = x_ref[...] + jnp.dot(merged_ref[...], wo_ref[...],
                                      preferred_element_type=F32)


def _outproj_call(attn, gb, ch, x2, wo, cw, ga, gc, *, seq, tm):
    n, d = x2.shape
    aw = attn.shape[1]
    cwid = gb.shape[1]
    tiles_per_seq = seq // tm
    blk8 = tm // SUBLANES
    last8 = n // SUBLANES - 1
    const = lambda i: (0, 0)
    kern = functools.partial(_outproj_kernel, tiles_per_seq=tiles_per_seq)
    return pl.pallas_call(
        kern,
        grid=(n // tm,),
        in_specs=[
            pl.BlockSpec((tm, aw), lambda i: (i, 0)),
            pl.BlockSpec((tm, cwid), lambda i: (i, 0)),
            pl.BlockSpec((tm, cwid), lambda i: (i, 0)),
            pl.BlockSpec((SUBLANES, cwid), lambda i: (jnp.maximum(i * blk8 - 1, 0), 0)),
            pl.BlockSpec((SUBLANES, cwid), lambda i: (jnp.minimum((i + 1) * blk8, last8), 0)),
            pl.BlockSpec((tm, d), lambda i: (i, 0)),
            pl.BlockSpec(wo.shape, const),
            pl.BlockSpec(cw.shape, const),
            pl.BlockSpec(ga.shape, const),
            pl.BlockSpec(gc.shape, const),
        ],
        out_specs=pl.BlockSpec((tm, d), lambda i: (i, 0)),
        out_shape=jax.ShapeDtypeStruct((n, d), F32),
        scratch_shapes=[pltpu.VMEM((tm, aw + cwid), BF16)],
        compiler_params=_compiler_params(("arbitrary",)),
        name="out_proj",
    )(attn, gb, ch, ch, ch, x2, wo, cw, ga, gc)


def _ffn_kernel(x_ref, xp_ref, xn_ref, g_ref, wup_ref, cw_ref, b_ref, wd_ref, gf_ref,
                o_ref, h_ref, a_ref, *, tiles_per_seq):
    i = pl.program_id(0)
    j = pl.program_id(1)
    tm = x_ref.shape[0]
    tn = wd_ref.shape[0]

    @pl.when(j == 0)
    def _():
        g = g_ref[...]
        x = x_ref[...]
        h_ref[0:tm, :] = _rms(x, g).astype(BF16)
        not_first = (i % tiles_per_seq != 0).astype(F32)
        not_last = (i % tiles_per_seq != tiles_per_seq - 1).astype(F32)
        halo = jnp.concatenate([_rms(xp_ref[...], g) * not_first,
                                _rms(xn_ref[...], g) * not_last], axis=0)
        h_ref[tm:tm + BF16_ROWS, :] = halo.astype(BF16)
        o_ref[...] = x

    h = h_ref[...]

    n_chains = tn // MXU_COLS
    for c in range(n_chains):
        cols = slice(2 * c * MXU_COLS, 2 * (c + 1) * MXU_COLS)
        a = jnp.dot(h, wup_ref[:, cols], preferred_element_type=F32)
        a_ref[c, SUBLANES:SUBLANES + tm, :] = a[0:tm]
        a_ref[c, 0:SUBLANES, :] = a[tm:tm + SUBLANES]
        a_ref[c, SUBLANES + tm:2 * SUBLANES + tm, :] = a[tm + SUBLANES:tm + 2 * SUBLANES]
    for c in range(n_chains):
        cols = slice(2 * c * MXU_COLS, 2 * (c + 1) * MXU_COLS)
        cw = cw_ref[:, cols]
        u = (a_ref[c, SUBLANES - 1:SUBLANES - 1 + tm, :] * cw[0:1, :]
             + a_ref[c, SUBLANES:SUBLANES + tm, :] * cw[1:2, :]
             + a_ref[c, SUBLANES + 1:SUBLANES + 1 + tm, :] * cw[2:3, :] + b_ref[:, cols])
        ug = u[:, :MXU_COLS]
        uv = u[:, MXU_COLS:]
        act = (ug * (1.0 / (1.0 + jnp.exp(-ug))) * uv).astype(BF16)
        o_ref[...] += jnp.dot(act, wd_ref[c * MXU_COLS:(c + 1) * MXU_COLS, :],
                              preferred_element_type=F32)

    @pl.when(j == pl.num_programs(1) - 1)
    def _():
        o_ref[...] = _rms(o_ref[...], gf_ref[...])


def _ffn_call(x1, g, wup, cw, b, wd, gf, *, seq, tm, tn):
    n, d = x1.shape
    dff = wd.shape[0]
    nj = dff // tn
    tiles_per_seq = seq // tm
    blk8 = tm // SUBLANES
    last8 = n // SUBLANES - 1
    const = lambda i, j: (0, 0)
    kern = functools.partial(_ffn_kernel, tiles_per_seq=tiles_per_seq)
    return pl.pallas_call(
        kern,
        grid=(n // tm, nj),
        in_specs=[
            pl.BlockSpec((tm, d), lambda i, j: (i, 0), pipeline_mode=pl.Buffered(1)),
            pl.BlockSpec((SUBLANES, d), lambda i, j: (jnp.maximum(i * blk8 - 1, 0), 0)),
            pl.BlockSpec((SUBLANES, d), lambda i, j: (jnp.minimum((i + 1) * blk8, last8), 0)),
            pl.BlockSpec(g.shape, const),
            pl.BlockSpec((d, 2 * tn), lambda i, j: (0, j)),
            pl.BlockSpec((3, 2 * tn), lambda i, j: (0, j)),
            pl.BlockSpec((1, 2 * tn), lambda i, j: (0, j)),
            pl.BlockSpec((tn, d), lambda i, j: (j, 0)),
            pl.BlockSpec(gf.shape, const),
        ],
        out_specs=pl.BlockSpec((tm, d), lambda i, j: (i, 0)),
        out_shape=jax.ShapeDtypeStruct((n, d), F32),
        scratch_shapes=[pltpu.VMEM((tm + BF16_ROWS, d), BF16),
                        pltpu.VMEM((tn // MXU_COLS, tm + 2 * SUBLANES, 2 * MXU_COLS), F32)],
        compiler_params=_compiler_params(("arbitrary", "arbitrary")),
        name="conv_ffn",
    )(x1, x1, x1, g, wup, cw, b, wd, gf)


def _rope_tables(seq):
    pos = jnp.arange(seq, dtype=F32)
    inv_freq = 1.0 / (ROPE_THETA ** (jnp.arange(0, QK_ROPE_DIM, 2, dtype=F32) / QK_ROPE_DIM))
    ang = pos[:, None] * inv_freq[None, :]
    c = jnp.cos(ang)
    s = jnp.sin(ang)
    return (jnp.concatenate([c, c, c, c], axis=-1),
            jnp.concatenate([-s, -s, s, s], axis=-1))


def _pair_gate_value_cols(w):
    r, two_dff = w.shape
    groups = two_dff // (2 * MXU_COLS)
    return w.reshape(r, 2, groups, MXU_COLS).transpose(0, 2, 1, 3).reshape(r, two_dff)


def _pad_rope_cols(w):
    z = jnp.zeros(w.shape[:-1] + (HALF_ROPE,), w.dtype)
    return jnp.concatenate([w[..., :HALF_ROPE], z, w[..., HALF_ROPE:], z], axis=-1)


def _layer(x2, p, cos_t, sin_t, *, batch, seq):
    d = x2.shape[1]
    q_rank = p["q_a_norm_g"].shape[0]
    kv_rank = p["kv_a_norm_g"].shape[0]
    lat = q_rank + kv_rank
    w_in = p["w_in"]
    wlat = jnp.concatenate(
        [w_in[:, :lat], _pad_rope_cols(w_in[:, lat:lat + QK_ROPE_DIM])], axis=1).astype(BF16)
    wbch = w_in[:, lat + QK_ROPE_DIM:].astype(BF16)
    wq = p["w_q_b"].reshape(q_rank, MLA_HEADS, QK_HEAD_DIM)
    wqb = jnp.concatenate([wq[..., :QK_NOPE_DIM], _pad_rope_cols(wq[..., QK_NOPE_DIM:])],
                          axis=-1).reshape(q_rank, MLA_HEADS * QK_PAD_DIM).astype(BF16)
    wkvb = p["w_kv_b"].astype(BF16)
    row = lambda v: v.reshape(1, -1)

    q, k, v = _latent_call(x2, row(p["attn_norm_g"]), wlat, row(p["q_a_norm_g"]),
                           row(p["kv_a_norm_g"]), wqb, wkvb, cos_t, sin_t,
                           batch=batch, seq=seq, tm=512)
    gb, ch = _gate_call(x2, row(p["attn_norm_g"]), wbch, tm=512)
    attn = _attn_call(q, k, v, tq=512, tk=1024).reshape(batch * seq, -1)
    x1 = _outproj_call(attn, gb, ch, x2, p["w_o"].astype(BF16), p["sc_conv_w"],
                       row(p["out_norm_attn_g"]), row(p["out_norm_conv_g"]), seq=seq, tm=512)
    return x1


def kernel(x, attn_norm_g, w_in, q_a_norm_g, kv_a_norm_g, w_q_b, w_kv_b, sc_conv_w,
           out_norm_attn_g, out_norm_conv_g, w_o, ffn_norm_g, w_ffn_up, ffn_conv_w,
           ffn_conv_b, w_ffn_down, final_norm_g):
    batch, seq, d = x.shape
    depth = w_in.shape[0]
    assert depth == 1, "the final RMSNorm is fused into the (single) ConvFFN layer"
    cos_t, sin_t = _rope_tables(seq)
    x2 = x.reshape(batch * seq, d)
    p = dict(attn_norm_g=attn_norm_g[0], w_in=w_in[0], q_a_norm_g=q_a_norm_g[0],
             kv_a_norm_g=kv_a_norm_g[0], w_q_b=w_q_b[0], w_kv_b=w_kv_b[0],
             sc_conv_w=sc_conv_w[0], out_norm_attn_g=out_norm_attn_g[0],
             out_norm_conv_g=out_norm_conv_g[0], w_o=w_o[0])
    x1 = _layer(x2, p, cos_t, sin_t, batch=batch, seq=seq)
    out = _ffn_call(x1, ffn_norm_g[0].reshape(1, -1),
                    _pair_gate_value_cols(w_ffn_up[0].astype(BF16)),
                    _pair_gate_value_cols(ffn_conv_w[0]),
                    _pair_gate_value_cols(ffn_conv_b[0].reshape(1, -1)),
                    w_ffn_down[0].astype(BF16),
                    final_norm_g.reshape(1, -1), seq=seq, tm=1024, tn=512)
    return out.reshape(batch, seq, d)
```

```python
import functools
import math

import jax
import jax.numpy as jnp
from jax import lax
from jax.experimental import pallas as pl
from jax.experimental.pallas import tpu as pltpu

F32 = jnp.float32
BF16 = jnp.bfloat16

EPS = 1e-6
MLA_HEADS = 8
QK_NOPE_DIM = 128
QK_ROPE_DIM = 64
V_HEAD_DIM = 128
QK_HEAD_DIM = QK_NOPE_DIM + QK_ROPE_DIM
ROPE_THETA = 10000.0

LANES = 128
SUBLANES = 8
BF16_ROWS = 16
MXU_COLS = 256
VMEM_LIMIT_BYTES = 56 * 1024 * 1024

QK_PAD_DIM = QK_NOPE_DIM + LANES
HALF_ROPE = QK_ROPE_DIM // 2


def _rms(xf, g):
    ms = jnp.mean(xf * xf, axis=-1, keepdims=True)
    return xf * lax.rsqrt(ms + EPS) * g


def _rope(z, cos_t, sin_t):
    return z * cos_t + pltpu.roll(z, LANES // 2, 1) * sin_t


def _compiler_params(semantics):
    return pltpu.CompilerParams(dimension_semantics=semantics,
                                vmem_limit_bytes=VMEM_LIMIT_BYTES)


def _latent_kernel(x_ref, g_ref, wlat_ref, gq_ref, gkv_ref, wqb_ref, wkvb_ref,
                   cos_ref, sin_ref, q_ref, k_ref, v_ref, *, q_rank, kv_rank, scale):
    h = _rms(x_ref[...], g_ref[...]).astype(BF16)
    z = jnp.dot(h, wlat_ref[...], preferred_element_type=F32)
    cq = _rms(z[:, :q_rank], gq_ref[...]).astype(BF16)
    ckv = _rms(z[:, q_rank:q_rank + kv_rank], gkv_ref[...]).astype(BF16)
    cos_t = cos_ref[...]
    sin_t = sin_ref[...]
    k_pe = _rope(z[:, q_rank + kv_rank:], cos_t, sin_t).astype(BF16)
    q = jnp.dot(cq, wqb_ref[...], preferred_element_type=F32)
    kv = jnp.dot(ckv, wkvb_ref[...], preferred_element_type=F32)
    for hh in range(MLA_HEADS):
        lo = hh * QK_PAD_DIM
        mid = lo + QK_NOPE_DIM
        hi = lo + QK_PAD_DIM
        q_ref[0, hh, :, 0:QK_NOPE_DIM] = (q[:, lo:mid] * scale).astype(BF16)
        q_ref[0, hh, :, QK_NOPE_DIM:QK_PAD_DIM] = (
            _rope(q[:, mid:hi], cos_t, sin_t) * scale).astype(BF16)
        k_ref[0, hh, :, 0:QK_NOPE_DIM] = kv[:, lo:mid].astype(BF16)
        k_ref[0, hh, :, QK_NOPE_DIM:QK_PAD_DIM] = k_pe
        v_ref[0, hh] = kv[:, mid:hi].astype(BF16)


def _latent_call(x2, g, wlat, gq, gkv, wqb, wkvb, cos_t, sin_t, *, batch, seq, tm):
    n, d = x2.shape
    q_rank = gq.shape[1]
    kv_rank = gkv.shape[1]
    tiles_per_seq = seq // tm
    const = lambda i: (0, 0)
    head_map = lambda i: (i // tiles_per_seq, 0, i % tiles_per_seq, 0)
    kern = functools.partial(_latent_kernel, q_rank=q_rank, kv_rank=kv_rank,
                             scale=1.0 / math.sqrt(QK_HEAD_DIM))
    return pl.pallas_call(
        kern,
        grid=(n // tm,),
        in_specs=[
            pl.BlockSpec((tm, d), lambda i: (i, 0)),
            pl.BlockSpec(g.shape, const),
            pl.BlockSpec(wlat.shape, const),
            pl.BlockSpec(gq.shape, const),
            pl.BlockSpec(gkv.shape, const),
            pl.BlockSpec(wqb.shape, const),
            pl.BlockSpec(wkvb.shape, const),
            pl.BlockSpec((tm, LANES), lambda i: (i % tiles_per_seq, 0)),
            pl.BlockSpec((tm, LANES), lambda i: (i % tiles_per_seq, 0)),
        ],
        out_specs=[
            pl.BlockSpec((1, MLA_HEADS, tm, QK_PAD_DIM), head_map),
            pl.BlockSpec((1, MLA_HEADS, tm, QK_PAD_DIM), head_map),
            pl.BlockSpec((1, MLA_HEADS, tm, V_HEAD_DIM), head_map),
        ],
        out_shape=[
            jax.ShapeDtypeStruct((batch, MLA_HEADS, seq, QK_PAD_DIM), BF16),
            jax.ShapeDtypeStruct((batch, MLA_HEADS, seq, QK_PAD_DIM), BF16),
            jax.ShapeDtypeStruct((batch, MLA_HEADS, seq, V_HEAD_DIM), BF16),
        ],
        compiler_params=_compiler_params(("arbitrary",)),
        name="latent_qkv",
    )(x2, g, wlat, gq, gkv, wqb, wkvb, cos_t, sin_t)


def _gate_kernel(x_ref, g_ref, wb_ref, wc_ref, wh_ref, gb_ref, ch_ref):
    h = _rms(x_ref[...], g_ref[...]).astype(BF16)
    gb_ref[...] = jnp.dot(h, wb_ref[...], preferred_element_type=F32).astype(BF16)
    zc = jnp.dot(h, wc_ref[...], preferred_element_type=F32)
    zh = jnp.dot(h, wh_ref[...], preferred_element_type=F32)
    ch_ref[...] = (zc * zh).astype(BF16)


def _gate_call(x2, g, wbch, *, tm):
    n, d = x2.shape
    cw = wbch.shape[1] // 3
    return pl.pallas_call(
        _gate_kernel,
        grid=(n // tm,),
        in_specs=[
            pl.BlockSpec((tm, d), lambda i: (i, 0)),
            pl.BlockSpec(g.shape, lambda i: (0, 0)),
            pl.BlockSpec((d, cw), lambda i: (0, 0)),
            pl.BlockSpec((d, cw), lambda i: (0, 1)),
            pl.BlockSpec((d, cw), lambda i: (0, 2)),
        ],
        out_specs=[
            pl.BlockSpec((tm, cw), lambda i: (i, 0)),
            pl.BlockSpec((tm, cw), lambda i: (i, 0)),
        ],
        out_shape=[
            jax.ShapeDtypeStruct((n, cw), BF16),
            jax.ShapeDtypeStruct((n, cw), BF16),
        ],
        compiler_params=_compiler_params(("arbitrary",)),
        name="gate_proj",
    )(x2, g, wbch, wbch, wbch)


def _attn_kernel(q_ref, k_ref, v_ref, o_ref, *, tq, tk):
    seq = k_ref.shape[2]

    def q_tile(qi, carry):
        row0 = pl.multiple_of(qi * tq, tq)
        q = q_ref[0, 0, pl.ds(row0, tq), :]
        m = jnp.full((tq, 1), -jnp.inf, F32)
        l = jnp.zeros((tq, 1), F32)
        acc = jnp.zeros((tq, V_HEAD_DIM), F32)
        for kc in range(seq // tk):
            k = k_ref[0, 0, kc * tk:(kc + 1) * tk, :]
            v = v_ref[0, 0, kc * tk:(kc + 1) * tk, :]
            s = lax.dot_general(q, k, (((1,), (1,)), ((), ())),
                                preferred_element_type=F32)
            m_new = jnp.maximum(m, jnp.max(s, axis=-1, keepdims=True))
            p = jnp.exp(s - m_new)
            alpha = jnp.exp(m - m_new)
            l = alpha * l + jnp.sum(p, axis=-1, keepdims=True)
            acc = alpha * acc + jnp.dot(p.astype(BF16), v, preferred_element_type=F32)
            m = m_new
        o_ref[0, pl.ds(row0, tq), :] = (acc / l).astype(o_ref.dtype)
        return carry

    lax.fori_loop(0, seq // tq, q_tile, 0)


def _attn_call(q, k, v, *, tq, tk):
    batch, heads, seq, _ = q.shape
    kern = functools.partial(_attn_kernel, tq=tq, tk=tk)
    return pl.pallas_call(
        kern,
        grid=(batch, heads),
        in_specs=[
            pl.BlockSpec((1, 1, seq, QK_PAD_DIM), lambda b, h: (b, h, 0, 0)),
            pl.BlockSpec((1, 1, seq, QK_PAD_DIM), lambda b, h: (b, h, 0, 0)),
            pl.BlockSpec((1, 1, seq, V_HEAD_DIM), lambda b, h: (b, h, 0, 0)),
        ],
        out_specs=pl.BlockSpec((1, seq, V_HEAD_DIM), lambda b, h: (b, 0, h)),
        out_shape=jax.ShapeDtypeStruct((batch, seq, heads * V_HEAD_DIM), BF16),
        compiler_params=_compiler_params(("arbitrary", "arbitrary")),
        name="mla_attention",
    )(q, k, v)


def _shift_rows(a, prev_row, next_row):
    tm = a.shape[0]
    row = lax.broadcasted_iota(jnp.int32, a.shape, 0)
    a_m1 = jnp.where(row == 0, prev_row, pltpu.roll(a, 1, 0))
    a_p1 = jnp.where(row == tm - 1, next_row, pltpu.roll(a, tm - 1, 0))
    return a_m1, a_p1


def _outproj_kernel(attn_ref, gb_ref, ch_ref, chp_ref, chn_ref, x_ref, wo_ref, cw_ref,
                    ga_ref, gc_ref, o_ref, merged_ref, *, tiles_per_seq):
    i = pl.program_id(0)
    aw = attn_ref.shape[1]
    merged_ref[:, :aw] = _rms(attn_ref[...].astype(F32), ga_ref[...]).astype(BF16)

    not_first = (i % tiles_per_seq != 0).astype(F32)
    not_last = (i % tiles_per_seq != tiles_per_seq - 1).astype(F32)
    ch = ch_ref[...].astype(F32)
    prev_row = chp_ref[SUBLANES - 1:SUBLANES, :].astype(F32) * not_first
    next_row = chn_ref[0:1, :].astype(F32) * not_last
    ch_m1, ch_p1 = _shift_rows(ch, prev_row, next_row)
    cw = cw_ref[...]
    y = gb_ref[...].astype(F32) * (ch_m1 * cw[0:1, :] + ch * cw[1:2, :] + ch_p1 * cw[2:3, :])
    merged_ref[:, aw:] = _rms(y, gc_ref[...]).astype(BF16)

    o_ref[...] = x_ref[...] + jnp.dot(merged_ref[...], wo_ref[...],
                                      preferred_element_type=F32)


def _outproj_call(attn, gb, ch, x2, wo, cw, ga, gc, *, seq, tm):
    n, d = x2.shape
    aw = attn.shape[1]
    cwid = gb.shape[1]
    tiles_per_seq = seq // tm
    blk8 = tm // SUBLANES
    last8 = n // SUBLANES - 1
    const = lambda i: (0, 0)
    kern = functools.partial(_outproj_kernel, tiles_per_seq=tiles_per_seq)
    return pl.pallas_call(
        kern,
        grid=(n // tm,),
        in_specs=[
            pl.BlockSpec((tm, aw), lambda i: (i, 0)),
            pl.BlockSpec((tm, cwid), lambda i: (i, 0)),
            pl.BlockSpec((tm, cwid), lambda i: (i, 0)),
            pl.BlockSpec((SUBLANES, cwid), lambda i: (jnp.maximum(i * blk8 - 1, 0), 0)),
            pl.BlockSpec((SUBLANES, cwid), lambda i: (jnp.minimum((i + 1) * blk8, last8), 0)),
            pl.BlockSpec((tm, d), lambda i: (i, 0)),
            pl.BlockSpec(wo.shape, const),
            pl.BlockSpec(cw.shape, const),
            pl.BlockSpec(ga.shape, const),
            pl.BlockSpec(gc.shape, const),
        ],
        out_specs=pl.BlockSpec((tm, d), lambda i: (i, 0)),
        out_shape=jax.ShapeDtypeStruct((n, d), F32),
        scratch_shapes=[pltpu.VMEM((tm, aw + cwid), BF16)],
        compiler_params=_compiler_params(("arbitrary",)),
        name="out_proj",
    )(attn, gb, ch, ch, ch, x2, wo, cw, ga, gc)


def _ffn_kernel(x_ref, xp_ref, xn_ref, g_ref, wg_ref, wv_ref, cw_ref, b_ref, wd_ref, gf_ref,
                o_ref, h_ref, a_ref, *, tiles_per_seq):
    i = pl.program_id(0)
    j = pl.program_id(1)
    tm = x_ref.shape[0]
    tn = wd_ref.shape[0]

    @pl.when(j == 0)
    def _():
        g = g_ref[...]
        x = x_ref[...]
        h_ref[0:tm, :] = _rms(x, g).astype(BF16)
        not_first = (i % tiles_per_seq != 0).astype(F32)
        not_last = (i % tiles_per_seq != tiles_per_seq - 1).astype(F32)
        halo = jnp.concatenate([_rms(xp_ref[...], g) * not_first,
                                _rms(xn_ref[...], g) * not_last], axis=0)
        h_ref[tm:tm + BF16_ROWS, :] = halo.astype(BF16)
        o_ref[...] = x

    h = h_ref[...]

    n_chains = tn // MXU_COLS
    for c in range(n_chains):
        grp = slice(c * MXU_COLS, (c + 1) * MXU_COLS)
        for half, w_ref in enumerate((wg_ref, wv_ref)):
            dst = slice(half * MXU_COLS, (half + 1) * MXU_COLS)
            a = jnp.dot(h, w_ref[:, grp], preferred_element_type=F32)
            a_ref[c, SUBLANES:SUBLANES + tm, dst] = a[0:tm]
            a_ref[c, 0:SUBLANES, dst] = a[tm:tm + SUBLANES]
            a_ref[c, SUBLANES + tm:2 * SUBLANES + tm, dst] = a[tm + SUBLANES:tm + 2 * SUBLANES]
    for c in range(n_chains):
        cols = slice(2 * c * MXU_COLS, 2 * (c + 1) * MXU_COLS)
        cw = cw_ref[:, cols]
        u = (a_ref[c, SUBLANES - 1:SUBLANES - 1 + tm, :] * cw[0:1, :]
             + a_ref[c, SUBLANES:SUBLANES + tm, :] * cw[1:2, :]
             + a_ref[c, SUBLANES + 1:SUBLANES + 1 + tm, :] * cw[2:3, :] + b_ref[:, cols])
        ug = u[:, :MXU_COLS]
        uv = u[:, MXU_COLS:]
        act = (ug * (1.0 / (1.0 + jnp.exp(-ug))) * uv).astype(BF16)
        o_ref[...] += jnp.dot(act, wd_ref[c * MXU_COLS:(c + 1) * MXU_COLS, :],
                              preferred_element_type=F32)

    @pl.when(j == pl.num_programs(1) - 1)
    def _():
        o_ref[...] = _rms(o_ref[...], gf_ref[...])


def _ffn_call(x1, g, wup, cw, b, wd, gf, *, seq, tm, tn):
    n, d = x1.shape
    dff = wd.shape[0]
    nj = dff // tn
    tiles_per_seq = seq // tm
    blk8 = tm // SUBLANES
    last8 = n // SUBLANES - 1
    const = lambda i, j: (0, 0)
    kern = functools.partial(_ffn_kernel, tiles_per_seq=tiles_per_seq)
    return pl.pallas_call(
        kern,
        grid=(n // tm, nj),
        in_specs=[
            pl.BlockSpec((tm, d), lambda i, j: (i, 0), pipeline_mode=pl.Buffered(1)),
            pl.BlockSpec((SUBLANES, d), lambda i, j: (jnp.maximum(i * blk8 - 1, 0), 0)),
            pl.BlockSpec((SUBLANES, d), lambda i, j: (jnp.minimum((i + 1) * blk8, last8), 0)),
            pl.BlockSpec(g.shape, const),
            pl.BlockSpec((d, tn), lambda i, j: (0, j)),
            pl.BlockSpec((d, tn), lambda i, j: (0, j + nj)),
            pl.BlockSpec((3, 2 * tn), lambda i, j: (0, j)),
            pl.BlockSpec((1, 2 * tn), lambda i, j: (0, j)),
            pl.BlockSpec((tn, d), lambda i, j: (j, 0)),
            pl.BlockSpec(gf.shape, const),
        ],
        out_specs=pl.BlockSpec((tm, d), lambda i, j: (i, 0)),
        out_shape=jax.ShapeDtypeStruct((n, d), F32),
        scratch_shapes=[pltpu.VMEM((tm + BF16_ROWS, d), BF16),
                        pltpu.VMEM((tn // MXU_COLS, tm + 2 * SUBLANES, 2 * MXU_COLS), F32)],
        compiler_params=_compiler_params(("arbitrary", "arbitrary")),
        name="conv_ffn",
    )(x1, x1, x1, g, wup, wup, cw, b, wd, gf)


def _rope_tables(seq):
    pos = jnp.arange(seq, dtype=F32)
    inv_freq = 1.0 / (ROPE_THETA ** (jnp.arange(0, QK_ROPE_DIM, 2, dtype=F32) / QK_ROPE_DIM))
    ang = pos[:, None] * inv_freq[None, :]
    c = jnp.cos(ang)
    s = jnp.sin(ang)
    return (jnp.concatenate([c, c, c, c], axis=-1),
            jnp.concatenate([-s, -s, s, s], axis=-1))


def _pair_gate_value_cols(w):
    r, two_dff = w.shape
    groups = two_dff // (2 * MXU_COLS)
    return w.reshape(r, 2, groups, MXU_COLS).transpose(0, 2, 1, 3).reshape(r, two_dff)


def _pad_rope_cols(w):
    z = jnp.zeros(w.shape[:-1] + (HALF_ROPE,), w.dtype)
    return jnp.concatenate([w[..., :HALF_ROPE], z, w[..., HALF_ROPE:], z], axis=-1)


def _layer(x2, p, cos_t, sin_t, *, batch, seq):
    d = x2.shape[1]
    q_rank = p["q_a_norm_g"].shape[0]
    kv_rank = p["kv_a_norm_g"].shape[0]
    lat = q_rank + kv_rank
    w_in = p["w_in"]
    wlat = jnp.concatenate(
        [w_in[:, :lat], _pad_rope_cols(w_in[:, lat:lat + QK_ROPE_DIM])], axis=1).astype(BF16)
    wbch = w_in[:, lat + QK_ROPE_DIM:].astype(BF16)
    wq = p["w_q_b"].reshape(q_rank, MLA_HEADS, QK_HEAD_DIM)
    wqb = jnp.concatenate([wq[..., :QK_NOPE_DIM], _pad_rope_cols(wq[..., QK_NOPE_DIM:])],
                          axis=-1).reshape(q_rank, MLA_HEADS * QK_PAD_DIM).astype(BF16)
    wkvb = p["w_kv_b"].astype(BF16)
    row = lambda v: v.reshape(1, -1)

    q, k, v = _latent_call(x2, row(p["attn_norm_g"]), wlat, row(p["q_a_norm_g"]),
                           row(p["kv_a_norm_g"]), wqb, wkvb, cos_t, sin_t,
                           batch=batch, seq=seq, tm=512)
    gb, ch = _gate_call(x2, row(p["attn_norm_g"]), wbch, tm=512)
    attn = _attn_call(q, k, v, tq=512, tk=1024).reshape(batch * seq, -1)
    x1 = _outproj_call(attn, gb, ch, x2, p["w_o"].astype(BF16), p["sc_conv_w"],
                       row(p["out_norm_attn_g"]), row(p["out_norm_conv_g"]), seq=seq, tm=512)
    return x1


def kernel(x, attn_norm_g, w_in, q_a_norm_g, kv_a_norm_g, w_q_b, w_kv_b, sc_conv_w,
           out_norm_attn_g, out_norm_conv_g, w_o, ffn_norm_g, w_ffn_up, ffn_conv_w,
           ffn_conv_b, w_ffn_down, final_norm_g):
    batch, seq, d = x.shape
    depth = w_in.shape[0]
    assert depth == 1, "the final RMSNorm is fused into the (single) ConvFFN layer"
    cos_t, sin_t = _rope_tables(seq)
    x2 = x.reshape(batch * seq, d)
    p = dict(attn_norm_g=attn_norm_g[0], w_in=w_in[0], q_a_norm_g=q_a_norm_g[0],
             kv_a_norm_g=kv_a_norm_g[0], w_q_b=w_q_b[0], w_kv_b=w_kv_b[0],
             sc_conv_w=sc_conv_w[0], out_norm_attn_g=out_norm_attn_g[0],
             out_norm_conv_g=out_norm_conv_g[0], w_o=w_o[0])
    x1 = _layer(x2, p, cos_t, sin_t, batch=batch, seq=seq)
    out = _ffn_call(x1, ffn_norm_g[0].reshape(1, -1),
                    w_ffn_up[0].astype(BF16),
                    _pair_gate_value_cols(ffn_conv_w[0]),
                    _pair_gate_value_cols(ffn_conv_b[0].reshape(1, -1)),
                    w_ffn_down[0].astype(BF16),
                    final_norm_g.reshape(1, -1), seq=seq, tm=1024, tn=512)
    return out.reshape(batch, seq, d)
```

```python
import functools
import math

import jax
import jax.numpy as jnp
from jax import lax
from jax.experimental import pallas as pl
from jax.experimental.pallas import tpu as pltpu

F32 = jnp.float32
BF16 = jnp.bfloat16

EPS = 1e-6
MLA_HEADS = 8
QK_NOPE_DIM = 128
QK_ROPE_DIM = 64
V_HEAD_DIM = 128
QK_HEAD_DIM = QK_NOPE_DIM + QK_ROPE_DIM
ROPE_THETA = 10000.0
LOG2_E = 1.4426950408889634

LANES = 128
SUBLANES = 8
BF16_ROWS = 16
MXU_COLS = 256
VMEM_LIMIT_BYTES = 56 * 1024 * 1024

QK_PAD_DIM = QK_NOPE_DIM + LANES
HALF_ROPE = QK_ROPE_DIM // 2


def _rms(xf, g):
    ms = jnp.mean(xf * xf, axis=-1, keepdims=True)
    return xf * lax.rsqrt(ms + EPS) * g


def _rope(z, cos_t, sin_t):
    return z * cos_t + pltpu.roll(z, LANES // 2, 1) * sin_t


def _compiler_params(semantics):
    return pltpu.CompilerParams(dimension_semantics=semantics,
                                vmem_limit_bytes=VMEM_LIMIT_BYTES)


def _latent_kernel(x_ref, g_ref, wlat_ref, gq_ref, gkv_ref, wqb_ref, wkvb_ref,
                   cos_ref, sin_ref, q_ref, k_ref, vt_ref, *, q_rank, kv_rank, scale):
    h = _rms(x_ref[...], g_ref[...]).astype(BF16)
    z = jnp.dot(h, wlat_ref[...], preferred_element_type=F32)
    cq = _rms(z[:, :q_rank], gq_ref[...]).astype(BF16)
    ckv = _rms(z[:, q_rank:q_rank + kv_rank], gkv_ref[...]).astype(BF16)
    cos_t = cos_ref[...]
    sin_t = sin_ref[...]
    k_pe = _rope(z[:, q_rank + kv_rank:], cos_t, sin_t).astype(BF16)
    q = jnp.dot(cq, wqb_ref[...], preferred_element_type=F32)
    kv = jnp.dot(ckv, wkvb_ref[...], preferred_element_type=F32)
    for hh in range(MLA_HEADS):
        lo = hh * QK_PAD_DIM
        mid = lo + QK_NOPE_DIM
        hi = lo + QK_PAD_DIM
        q_ref[0, hh, :, 0:QK_NOPE_DIM] = (q[:, lo:mid] * scale).astype(BF16)
        q_ref[0, hh, :, QK_NOPE_DIM:QK_PAD_DIM] = (
            _rope(q[:, mid:hi], cos_t, sin_t) * scale).astype(BF16)
        k_ref[0, hh, :, 0:QK_NOPE_DIM] = kv[:, lo:mid].astype(BF16)
        k_ref[0, hh, :, QK_NOPE_DIM:QK_PAD_DIM] = k_pe
        vt_ref[0, hh] = kv[:, mid:hi].T.astype(BF16)


def _latent_call(x2, g, wlat, gq, gkv, wqb, wkvb, cos_t, sin_t, *, batch, seq, tm):
    n, d = x2.shape
    q_rank = gq.shape[1]
    kv_rank = gkv.shape[1]
    tiles_per_seq = seq // tm
    const = lambda i: (0, 0)
    head_map = lambda i: (i // tiles_per_seq, 0, i % tiles_per_seq, 0)
    kern = functools.partial(_latent_kernel, q_rank=q_rank, kv_rank=kv_rank,
                             scale=LOG2_E / math.sqrt(QK_HEAD_DIM))
    return pl.pallas_call(
        kern,
        grid=(n // tm,),
        in_specs=[
            pl.BlockSpec((tm, d), lambda i: (i, 0)),
            pl.BlockSpec(g.shape, const),
            pl.BlockSpec(wlat.shape, const),
            pl.BlockSpec(gq.shape, const),
            pl.BlockSpec(gkv.shape, const),
            pl.BlockSpec(wqb.shape, const),
            pl.BlockSpec(wkvb.shape, const),
            pl.BlockSpec((tm, LANES), lambda i: (i % tiles_per_seq, 0)),
            pl.BlockSpec((tm, LANES), lambda i: (i % tiles_per_seq, 0)),
        ],
        out_specs=[
            pl.BlockSpec((1, MLA_HEADS, tm, QK_PAD_DIM), head_map),
            pl.BlockSpec((1, MLA_HEADS, tm, QK_PAD_DIM), head_map),
            pl.BlockSpec((1, MLA_HEADS, V_HEAD_DIM, tm),
                         lambda i: (i // tiles_per_seq, 0, 0, i % tiles_per_seq)),
        ],
        out_shape=[
            jax.ShapeDtypeStruct((batch, MLA_HEADS, seq, QK_PAD_DIM), BF16),
            jax.ShapeDtypeStruct((batch, MLA_HEADS, seq, QK_PAD_DIM), BF16),
            jax.ShapeDtypeStruct((batch, MLA_HEADS, V_HEAD_DIM, seq), BF16),
        ],
        compiler_params=_compiler_params(("arbitrary",)),
        name="latent_qkv",
    )(x2, g, wlat, gq, gkv, wqb, wkvb, cos_t, sin_t)


def _gate_kernel(x_ref, g_ref, wb_ref, wc_ref, wh_ref, gb_ref, ch_ref):
    h = _rms(x_ref[...], g_ref[...]).astype(BF16)
    gb_ref[...] = jnp.dot(h, wb_ref[...], preferred_element_type=F32).astype(BF16)
    zc = jnp.dot(h, wc_ref[...], preferred_element_type=F32)
    zh = jnp.dot(h, wh_ref[...], preferred_element_type=F32)
    ch_ref[...] = (zc * zh).astype(BF16)


def _gate_call(x2, g, wbch, *, tm):
    n, d = x2.shape
    cw = wbch.shape[1] // 3
    return pl.pallas_call(
        _gate_kernel,
        grid=(n // tm,),
        in_specs=[
            pl.BlockSpec((tm, d), lambda i: (i, 0)),
            pl.BlockSpec(g.shape, lambda i: (0, 0)),
            pl.BlockSpec((d, cw), lambda i: (0, 0)),
            pl.BlockSpec((d, cw), lambda i: (0, 1)),
            pl.BlockSpec((d, cw), lambda i: (0, 2)),
        ],
        out_specs=[
            pl.BlockSpec((tm, cw), lambda i: (i, 0)),
            pl.BlockSpec((tm, cw), lambda i: (i, 0)),
        ],
        out_shape=[
            jax.ShapeDtypeStruct((n, cw), BF16),
            jax.ShapeDtypeStruct((n, cw), BF16),
        ],
        compiler_params=_compiler_params(("arbitrary",)),
        name="gate_proj",
    )(x2, g, wbch, wbch, wbch)


SCORE_LOOKAHEAD = 2


def _attn_kernel(q_ref, k_ref, vt_ref, o_ref, *, tq, tk):
    seq = k_ref.shape[2]

    def q_tile(qi, carry):
        row0 = pl.multiple_of(qi * tq, tq)
        q = q_ref[0, 0, pl.ds(row0, tq), :]
        m = jnp.full((1, tq), -jnp.inf, F32)
        l = jnp.zeros((1, tq), F32)
        acc = jnp.zeros((V_HEAD_DIM, tq), F32)

        def scores(kc):
            k = k_ref[0, 0, kc * tk:(kc + 1) * tk, :]
            return lax.dot_general(k, q, (((1,), (1,)), ((), ())),
                                   preferred_element_type=F32)

        n_chunks = seq // tk
        ahead = [scores(kc) for kc in range(min(SCORE_LOOKAHEAD, n_chunks))]
        for kc in range(n_chunks):
            vt = vt_ref[0, 0, :, kc * tk:(kc + 1) * tk]
            st = ahead.pop(0)
            if kc + SCORE_LOOKAHEAD < n_chunks:
                ahead.append(scores(kc + SCORE_LOOKAHEAD))
            m_new = jnp.maximum(m, jnp.max(st, axis=0, keepdims=True))
            p = jnp.exp2(st - m_new)
            alpha = jnp.exp2(m - m_new)
            l = alpha * l + jnp.sum(p, axis=0, keepdims=True)
            acc = alpha * acc + jnp.dot(vt, p.astype(BF16), preferred_element_type=F32)
            m = m_new
        o_ref[0, pl.ds(row0, tq), :] = (acc / l).T.astype(o_ref.dtype)
        return carry

    lax.fori_loop(0, seq // tq, q_tile, 0)


def _attn_call(q, k, v, *, tq, tk):
    batch, heads, seq, _ = q.shape
    kern = functools.partial(_attn_kernel, tq=tq, tk=tk)
    return pl.pallas_call(
        kern,
        grid=(batch, heads),
        in_specs=[
            pl.BlockSpec((1, 1, seq, QK_PAD_DIM), lambda b, h: (b, h, 0, 0)),
            pl.BlockSpec((1, 1, seq, QK_PAD_DIM), lambda b, h: (b, h, 0, 0)),
            pl.BlockSpec((1, 1, V_HEAD_DIM, seq), lambda b, h: (b, h, 0, 0)),
        ],
        out_specs=pl.BlockSpec((1, seq, V_HEAD_DIM), lambda b, h: (b, 0, h)),
        out_shape=jax.ShapeDtypeStruct((batch, seq, heads * V_HEAD_DIM), BF16),
        compiler_params=_compiler_params(("arbitrary", "arbitrary")),
        name="mla_attention",
    )(q, k, v)


def _shift_rows(a, prev_row, next_row):
    tm = a.shape[0]
    row = lax.broadcasted_iota(jnp.int32, a.shape, 0)
    a_m1 = jnp.where(row == 0, prev_row, pltpu.roll(a, 1, 0))
    a_p1 = jnp.where(row == tm - 1, next_row, pltpu.roll(a, tm - 1, 0))
    return a_m1, a_p1


def _outproj_kernel(attn_ref, gb_ref, ch_ref, chp_ref, chn_ref, x_ref, wo_ref, cw_ref,
                    ga_ref, gc_ref, o_ref, merged_ref, *, tiles_per_seq):
    i = pl.program_id(0)
    aw = attn_ref.shape[1]
    merged_ref[:, :aw] = _rms(attn_ref[...].astype(F32), ga_ref[...]).astype(BF16)

    not_first = (i % tiles_per_seq != 0).astype(F32)
    not_last = (i % tiles_per_seq != tiles_per_seq - 1).astype(F32)
    ch = ch_ref[...].astype(F32)
    prev_row = chp_ref[SUBLANES - 1:SUBLANES, :].astype(F32) * not_first
    next_row = chn_ref[0:1, :].astype(F32) * not_last
    ch_m1, ch_p1 = _shift_rows(ch, prev_row, next_row)
    cw = cw_ref[...]
    y = gb_ref[...].astype(F32) * (ch_m1 * cw[0:1, :] + ch * cw[1:2, :] + ch_p1 * cw[2:3, :])
    merged_ref[:, aw:] = _rms(y, gc_ref[...]).astype(BF16)

    o_ref[...] = x_ref[...] + jnp.dot(merged_ref[...], wo_ref[...],
                                      preferred_element_type=F32)


def _outproj_call(attn, gb, ch, x2, wo, cw, ga, gc, *, seq, tm):
    n, d = x2.shape
    aw = attn.shape[1]
    cwid = gb.shape[1]
    tiles_per_seq = seq // tm
    blk8 = tm // SUBLANES
    last8 = n // SUBLANES - 1
    const = lambda i: (0, 0)
    kern = functools.partial(_outproj_kernel, tiles_per_seq=tiles_per_seq)
    return pl.pallas_call(
        kern,
        grid=(n // tm,),
        in_specs=[
            pl.BlockSpec((tm, aw), lambda i: (i, 0)),
            pl.BlockSpec((tm, cwid), lambda i: (i, 0)),
            pl.BlockSpec((tm, cwid), lambda i: (i, 0)),
            pl.BlockSpec((SUBLANES, cwid), lambda i: (jnp.maximum(i * blk8 - 1, 0), 0)),
            pl.BlockSpec((SUBLANES, cwid), lambda i: (jnp.minimum((i + 1) * blk8, last8), 0)),
            pl.BlockSpec((tm, d), lambda i: (i, 0)),
            pl.BlockSpec(wo.shape, const),
            pl.BlockSpec(cw.shape, const),
            pl.BlockSpec(ga.shape, const),
            pl.BlockSpec(gc.shape, const),
        ],
        out_specs=pl.BlockSpec((tm, d), lambda i: (i, 0)),
        out_shape=jax.ShapeDtypeStruct((n, d), F32),
        scratch_shapes=[pltpu.VMEM((tm, aw + cwid), BF16)],
        compiler_params=_compiler_params(("arbitrary",)),
        name="out_proj",
    )(attn, gb, ch, ch, ch, x2, wo, cw, ga, gc)


def _ffn_kernel(x_ref, xp_ref, xn_ref, g_ref, wg_ref, wv_ref, cw_ref, b_ref, wd_ref, gf_ref,
                o_ref, h_ref, a_ref, *, tiles_per_seq):
    i = pl.program_id(0)
    j = pl.program_id(1)
    tm = x_ref.shape[0]
    tn = wd_ref.shape[0]

    @pl.when(j == 0)
    def _():
        g = g_ref[...]
        x = x_ref[...]
        h_ref[0:tm, :] = _rms(x, g).astype(BF16)
        not_first = (i % tiles_per_seq != 0).astype(F32)
        not_last = (i % tiles_per_seq != tiles_per_seq - 1).astype(F32)
        halo = jnp.concatenate([_rms(xp_ref[...], g) * not_first,
                                _rms(xn_ref[...], g) * not_last], axis=0)
        h_ref[tm:tm + BF16_ROWS, :] = halo.astype(BF16)
        o_ref[...] = x

    h = h_ref[...]

    n_chains = tn // MXU_COLS
    for c in range(n_chains):
        grp = slice(c * MXU_COLS, (c + 1) * MXU_COLS)
        for half, w_ref in enumerate((wg_ref, wv_ref)):
            dst = slice(half * MXU_COLS, (half + 1) * MXU_COLS)
            a = jnp.dot(h, w_ref[:, grp], preferred_element_type=F32)
            a_ref[c, SUBLANES:SUBLANES + tm, dst] = a[0:tm]
            a_ref[c, 0:SUBLANES, dst] = a[tm:tm + SUBLANES]
            a_ref[c, SUBLANES + tm:2 * SUBLANES + tm, dst] = a[tm + SUBLANES:tm + 2 * SUBLANES]
    for c in range(n_chains):
        cols = slice(2 * c * MXU_COLS, 2 * (c + 1) * MXU_COLS)
        cw = cw_ref[:, cols]
        u = (a_ref[c, SUBLANES - 1:SUBLANES - 1 + tm, :] * cw[0:1, :]
             + a_ref[c, SUBLANES:SUBLANES + tm, :] * cw[1:2, :]
             + a_ref[c, SUBLANES + 1:SUBLANES + 1 + tm, :] * cw[2:3, :] + b_ref[:, cols])
        ug = u[:, :MXU_COLS]
        uv = u[:, MXU_COLS:]
        act = (ug * (1.0 / (1.0 + jnp.exp(-ug))) * uv).astype(BF16)
        o_ref[...] += jnp.dot(act, wd_ref[c * MXU_COLS:(c + 1) * MXU_COLS, :],
                              preferred_element_type=F32)

    @pl.when(j == pl.num_programs(1) - 1)
    def _():
        o_ref[...] = _rms(o_ref[...], gf_ref[...])


def _ffn_call(x1, g, wup, cw, b, wd, gf, *, seq, tm, tn):
    n, d = x1.shape
    dff = wd.shape[0]
    nj = dff // tn
    tiles_per_seq = seq // tm
    blk8 = tm // SUBLANES
    last8 = n // SUBLANES - 1
    const = lambda i, j: (0, 0)
    kern = functools.partial(_ffn_kernel, tiles_per_seq=tiles_per_seq)
    return pl.pallas_call(
        kern,
        grid=(n // tm, nj),
        in_specs=[
            pl.BlockSpec((tm, d), lambda i, j: (i, 0), pipeline_mode=pl.Buffered(1)),
            pl.BlockSpec((SUBLANES, d), lambda i, j: (jnp.maximum(i * blk8 - 1, 0), 0)),
            pl.BlockSpec((SUBLANES, d), lambda i, j: (jnp.minimum((i + 1) * blk8, last8), 0)),
            pl.BlockSpec(g.shape, const),
            pl.BlockSpec((d, tn), lambda i, j: (0, j)),
            pl.BlockSpec((d, tn), lambda i, j: (0, j + nj)),
            pl.BlockSpec((3, 2 * tn), lambda i, j: (0, j)),
            pl.BlockSpec((1, 2 * tn), lambda i, j: (0, j)),
            pl.BlockSpec((tn, d), lambda i, j: (j, 0)),
            pl.BlockSpec(gf.shape, const),
        ],
        out_specs=pl.BlockSpec((tm, d), lambda i, j: (i, 0)),
        out_shape=jax.ShapeDtypeStruct((n, d), F32),
        scratch_shapes=[pltpu.VMEM((tm + BF16_ROWS, d), BF16),
                        pltpu.VMEM((tn // MXU_COLS, tm + 2 * SUBLANES, 2 * MXU_COLS), F32)],
        compiler_params=_compiler_params(("arbitrary", "arbitrary")),
        name="conv_ffn",
    )(x1, x1, x1, g, wup, wup, cw, b, wd, gf)


def _rope_tables(seq):
    pos = jnp.arange(seq, dtype=F32)
    inv_freq = 1.0 / (ROPE_THETA ** (jnp.arange(0, QK_ROPE_DIM, 2, dtype=F32) / QK_ROPE_DIM))
    ang = pos[:, None] * inv_freq[None, :]
    c = jnp.cos(ang)
    s = jnp.sin(ang)
    return (jnp.concatenate([c, c, c, c], axis=-1),
            jnp.concatenate([-s, -s, s, s], axis=-1))


def _pair_gate_value_cols(w):
    r, two_dff = w.shape
    groups = two_dff // (2 * MXU_COLS)
    return w.reshape(r, 2, groups, MXU_COLS).transpose(0, 2, 1, 3).reshape(r, two_dff)


def _pad_rope_cols(w):
    z = jnp.zeros(w.shape[:-1] + (HALF_ROPE,), w.dtype)
    return jnp.concatenate([w[..., :HALF_ROPE], z, w[..., HALF_ROPE:], z], axis=-1)


def _layer(x2, p, cos_t, sin_t, *, batch, seq):
    d = x2.shape[1]
    q_rank = p["q_a_norm_g"].shape[0]
    kv_rank = p["kv_a_norm_g"].shape[0]
    lat = q_rank + kv_rank
    w_in = p["w_in"]
    wlat = jnp.concatenate(
        [w_in[:, :lat], _pad_rope_cols(w_in[:, lat:lat + QK_ROPE_DIM])], axis=1).astype(BF16)
    wbch = w_in[:, lat + QK_ROPE_DIM:].astype(BF16)
    wq = p["w_q_b"].reshape(q_rank, MLA_HEADS, QK_HEAD_DIM)
    wqb = jnp.concatenate([wq[..., :QK_NOPE_DIM], _pad_rope_cols(wq[..., QK_NOPE_DIM:])],
                          axis=-1).reshape(q_rank, MLA_HEADS * QK_PAD_DIM).astype(BF16)
    wkvb = p["w_kv_b"].astype(BF16)
    row = lambda v: v.reshape(1, -1)

    q, k, v = _latent_call(x2, row(p["attn_norm_g"]), wlat, row(p["q_a_norm_g"]),
                           row(p["kv_a_norm_g"]), wqb, wkvb, cos_t, sin_t,
                           batch=batch, seq=seq, tm=512)
    gb, ch = _gate_call(x2, row(p["attn_norm_g"]), wbch, tm=512)
    attn = _attn_call(q, k, v, tq=512, tk=1024).reshape(batch * seq, -1)
    x1 = _outproj_call(attn, gb, ch, x2, p["w_o"].astype(BF16), p["sc_conv_w"],
                       row(p["out_norm_attn_g"]), row(p["out_norm_conv_g"]), seq=seq, tm=512)
    return x1


def kernel(x, attn_norm_g, w_in, q_a_norm_g, kv_a_norm_g, w_q_b, w_kv_b, sc_conv_w,
           out_norm_attn_g, out_norm_conv_g, w_o, ffn_norm_g, w_ffn_up, ffn_conv_w,
           ffn_conv_b, w_ffn_down, final_norm_g):
    batch, seq, d = x.shape
    depth = w_in.shape[0]
    assert depth == 1, "the final RMSNorm is fused into the (single) ConvFFN layer"
    cos_t, sin_t = _rope_tables(seq)
    x2 = x.reshape(batch * seq, d)
    p = dict(attn_norm_g=attn_norm_g[0], w_in=w_in[0], q_a_norm_g=q_a_norm_g[0],
             kv_a_norm_g=kv_a_norm_g[0], w_q_b=w_q_b[0], w_kv_b=w_kv_b[0],
             sc_conv_w=sc_conv_w[0], out_norm_attn_g=out_norm_attn_g[0],
             out_norm_conv_g=out_norm_conv_g[0], w_o=w_o[0])
    x1 = _layer(x2, p, cos_t, sin_t, batch=batch, seq=seq)
    out = _ffn_call(x1, ffn_norm_g[0].reshape(1, -1),
                    w_ffn_up[0].astype(BF16),
                    _pair_gate_value_cols(ffn_conv_w[0]),
                    _pair_gate_value_cols(ffn_conv_b[0].reshape(1, -1)),
                    w_ffn_down[0].astype(BF16),
                    final_norm_g.reshape(1, -1), seq=seq, tm=1024, tn=512)
    return out.reshape(batch, seq, d)
```

```python
import functools
import math

import jax
import jax.numpy as jnp
from jax import lax
from jax.experimental import pallas as pl
from jax.experimental.pallas import tpu as pltpu

F32 = jnp.float32
BF16 = jnp.bfloat16

EPS = 1e-6
MLA_HEADS = 8
QK_NOPE_DIM = 128
QK_ROPE_DIM = 64
V_HEAD_DIM = 128
QK_HEAD_DIM = QK_NOPE_DIM + QK_ROPE_DIM
ROPE_THETA = 10000.0
LOG2_E = 1.4426950408889634

LANES = 128
SUBLANES = 8
BF16_ROWS = 16
MXU_COLS = 256
VMEM_LIMIT_BYTES = 56 * 1024 * 1024

QK_PAD_DIM = QK_NOPE_DIM + LANES
HALF_ROPE = QK_ROPE_DIM // 2
V_EXT_ROWS = V_HEAD_DIM + BF16_ROWS


def _rms(xf, g):
    ms = jnp.mean(xf * xf, axis=-1, keepdims=True)
    return xf * lax.rsqrt(ms + EPS) * g


def _rope(z, cos_t, nsin_t, psin_t):
    x2_under_x1 = pltpu.roll(z, LANES - HALF_ROPE, 1)
    x1_under_x2 = pltpu.roll(z, HALF_ROPE, 1)
    return z * cos_t + x2_under_x1 * nsin_t + x1_under_x2 * psin_t


def _compiler_params(semantics, flags=None):
    return pltpu.CompilerParams(dimension_semantics=semantics,
                                vmem_limit_bytes=VMEM_LIMIT_BYTES, flags=flags)


def _latent_kernel(x_ref, g_ref, wlat_ref, gq_ref, gkv_ref, wqb_ref, wkvb_ref,
                   cos_ref, nsin_ref, psin_ref, q_ref, k_ref, vt_ref, *, q_rank, kv_rank, scale):
    h = _rms(x_ref[...], g_ref[...]).astype(BF16)
    z = jnp.dot(h, wlat_ref[...], preferred_element_type=F32)
    cq = _rms(z[:, :q_rank], gq_ref[...]).astype(BF16)
    ckv = _rms(z[:, q_rank:q_rank + kv_rank], gkv_ref[...]).astype(BF16)
    rope_t = (cos_ref[...], nsin_ref[...], psin_ref[...])
    k_pe = _rope(z[:, q_rank + kv_rank:], *rope_t).astype(BF16)
    q = jnp.dot(cq, wqb_ref[...], preferred_element_type=F32)
    kv = jnp.dot(ckv, wkvb_ref[...], preferred_element_type=F32)
    for hh in range(MLA_HEADS):
        lo = hh * QK_PAD_DIM
        mid = lo + QK_NOPE_DIM
        hi = lo + QK_PAD_DIM
        q_ref[0, hh, :, 0:QK_NOPE_DIM] = (q[:, lo:mid] * scale).astype(BF16)
        q_ref[0, hh, :, QK_NOPE_DIM:QK_PAD_DIM] = (
            _rope(q[:, mid:hi], *rope_t) * scale).astype(BF16)
        k_ref[0, hh, :, 0:QK_NOPE_DIM] = kv[:, lo:mid].astype(BF16)
        k_ref[0, hh, :, QK_NOPE_DIM:QK_PAD_DIM] = k_pe
        vt_ref[0, hh, 0:V_HEAD_DIM, :] = kv[:, mid:hi].T.astype(BF16)
        vt_ref[0, hh, V_HEAD_DIM:V_EXT_ROWS, :] = jnp.ones(
            (V_EXT_ROWS - V_HEAD_DIM, kv.shape[0]), BF16)


def _latent_call(x2, g, w_in, gq, gkv, wqb, wkvb, rope_t, *, batch, seq, tm):
    n, d = x2.shape
    q_rank = gq.shape[1]
    kv_rank = gkv.shape[1]
    tiles_per_seq = seq // tm
    const = lambda i: (0, 0)
    rope_spec = pl.BlockSpec((tm, LANES), lambda i: (i % tiles_per_seq, 0))
    head_map = lambda i: (i // tiles_per_seq, 0, i % tiles_per_seq, 0)
    kern = functools.partial(_latent_kernel, q_rank=q_rank, kv_rank=kv_rank,
                             scale=LOG2_E / math.sqrt(QK_HEAD_DIM))
    return pl.pallas_call(
        kern,
        grid=(n // tm,),
        in_specs=[
            pl.BlockSpec((tm, d), lambda i: (i, 0)),
            pl.BlockSpec(g.shape, const),
            pl.BlockSpec((d, q_rank + kv_rank + LANES), const),
            pl.BlockSpec(gq.shape, const),
            pl.BlockSpec(gkv.shape, const),
            pl.BlockSpec(wqb.shape, const),
            pl.BlockSpec(wkvb.shape, const),
            rope_spec, rope_spec, rope_spec,
        ],
        out_specs=[
            pl.BlockSpec((1, MLA_HEADS, tm, QK_PAD_DIM), head_map),
            pl.BlockSpec((1, MLA_HEADS, tm, QK_PAD_DIM), head_map),
            pl.BlockSpec((1, MLA_HEADS, V_EXT_ROWS, tm),
                         lambda i: (i // tiles_per_seq, 0, 0, i % tiles_per_seq)),
        ],
        out_shape=[
            jax.ShapeDtypeStruct((batch, MLA_HEADS, seq, QK_PAD_DIM), BF16),
            jax.ShapeDtypeStruct((batch, MLA_HEADS, seq, QK_PAD_DIM), BF16),
            jax.ShapeDtypeStruct((batch, MLA_HEADS, V_EXT_ROWS, seq), BF16),
        ],
        compiler_params=_compiler_params(("arbitrary",)),
        name="latent_qkv",
    )(x2, g, w_in, gq, gkv, wqb, wkvb, *rope_t)


def _gate_kernel(x_ref, g_ref, wb_ref, wc_ref, wh_ref, gb_ref, ch_ref):
    h = _rms(x_ref[...], g_ref[...]).astype(BF16)
    gb_ref[...] = jnp.dot(h, wb_ref[...], preferred_element_type=F32).astype(BF16)
    zc = jnp.dot(h, wc_ref[...], preferred_element_type=F32)
    zh = jnp.dot(h, wh_ref[...], preferred_element_type=F32)
    ch_ref[...] = (zc * zh).astype(BF16)


def _gate_call(x2, g, wbch, *, tm):
    n, d = x2.shape
    cw = wbch.shape[1] // 3
    return pl.pallas_call(
        _gate_kernel,
        grid=(n // tm,),
        in_specs=[
            pl.BlockSpec((tm, d), lambda i: (i, 0)),
            pl.BlockSpec(g.shape, lambda i: (0, 0)),
            pl.BlockSpec((d, cw), lambda i: (0, 0)),
            pl.BlockSpec((d, cw), lambda i: (0, 1)),
            pl.BlockSpec((d, cw), lambda i: (0, 2)),
        ],
        out_specs=[
            pl.BlockSpec((tm, cw), lambda i: (i, 0)),
            pl.BlockSpec((tm, cw), lambda i: (i, 0)),
        ],
        out_shape=[
            jax.ShapeDtypeStruct((n, cw), BF16),
            jax.ShapeDtypeStruct((n, cw), BF16),
        ],
        compiler_params=_compiler_params(("arbitrary",)),
        name="gate_proj",
    )(x2, g, wbch, wbch, wbch)


SCORE_LOOKAHEAD = 2


def _attn_kernel(q_ref, k_ref, vt_ref, o_ref, *, tq, tk, q_group):
    seq = k_ref.shape[2]
    n_chunks = seq // tk
    n_groups = seq // (tq * q_group)
    steps = [(qi, kc) for qi in range(q_group) for kc in range(n_chunks)]

    def group(gi, carry):
        def q_rows(qi):
            if n_groups == 1:
                return pl.ds(qi * tq, tq)
            return pl.ds(pl.multiple_of((gi * q_group + qi) * tq, tq), tq)

        def scores(step):
            qi, kc = step
            q = q_ref[0, 0, q_rows(qi), :]
            k = k_ref[0, 0, kc * tk:(kc + 1) * tk, :]
            return lax.dot_general(k, q, (((1,), (1,)), ((), ())),
                                   preferred_element_type=F32)

        ahead = [scores(s) for s in steps[:SCORE_LOOKAHEAD]]
        m = acc = None
        for idx, (qi, kc) in enumerate(steps):
            st = ahead.pop(0)
            if idx + SCORE_LOOKAHEAD < len(steps):
                ahead.append(scores(steps[idx + SCORE_LOOKAHEAD]))
            if kc == 0:
                m = jnp.full((1, tq), -jnp.inf, F32)
                acc = jnp.zeros((V_EXT_ROWS, tq), F32)
            m_new = jnp.maximum(m, jnp.max(st, axis=0, keepdims=True))
            acc = jnp.exp2(m - m_new) * acc
            p = jnp.exp2(st - m_new).astype(BF16)
            vt = vt_ref[0, 0, :, kc * tk:(kc + 1) * tk]
            acc = acc + jnp.dot(vt, p, preferred_element_type=F32)
            m = m_new
            if kc == n_chunks - 1:
                out_t = acc[0:V_HEAD_DIM] / acc[V_HEAD_DIM:V_HEAD_DIM + 1]
                o_ref[0, q_rows(qi), :] = out_t.T.astype(o_ref.dtype)
        return carry

    if n_groups == 1:
        group(0, 0)
    else:
        lax.fori_loop(0, n_groups, group, 0)


def _attn_call(q, k, v, *, tq, tk, q_group):
    batch, heads, seq, _ = q.shape
    kern = functools.partial(_attn_kernel, tq=tq, tk=tk, q_group=q_group)
    return pl.pallas_call(
        kern,
        grid=(batch, heads),
        in_specs=[
            pl.BlockSpec((1, 1, seq, QK_PAD_DIM), lambda b, h: (b, h, 0, 0)),
            pl.BlockSpec((1, 1, seq, QK_PAD_DIM), lambda b, h: (b, h, 0, 0)),
            pl.BlockSpec((1, 1, V_EXT_ROWS, seq), lambda b, h: (b, h, 0, 0)),
        ],
        out_specs=pl.BlockSpec((1, seq, V_HEAD_DIM), lambda b, h: (b, 0, h)),
        out_shape=jax.ShapeDtypeStruct((batch, seq, heads * V_HEAD_DIM), BF16),
        compiler_params=_compiler_params(("arbitrary", "arbitrary")),
        name="mla_attention",
    )(q, k, v)


def _shift_rows(a, prev_row, next_row):
    tm = a.shape[0]
    row = lax.broadcasted_iota(jnp.int32, a.shape, 0)
    a_m1 = jnp.where(row == 0, prev_row, pltpu.roll(a, 1, 0))
    a_p1 = jnp.where(row == tm - 1, next_row, pltpu.roll(a, tm - 1, 0))
    return a_m1, a_p1


def _outproj_kernel(attn_ref, gb_ref, ch_ref, chp_ref, chn_ref, x_ref, wo_ref, cw_ref,
                    ga_ref, gc_ref, o_ref, merged_ref, *, tiles_per_seq):
    i = pl.program_id(0)
    aw = attn_ref.shape[1]
    merged_ref[:, :aw] = _rms(attn_ref[...].astype(F32), ga_ref[...]).astype(BF16)

    not_first = (i % tiles_per_seq != 0).astype(F32)
    not_last = (i % tiles_per_seq != tiles_per_seq - 1).astype(F32)
    ch = ch_ref[...].astype(F32)
    prev_row = chp_ref[SUBLANES - 1:SUBLANES, :].astype(F32) * not_first
    next_row = chn_ref[0:1, :].astype(F32) * not_last
    ch_m1, ch_p1 = _shift_rows(ch, prev_row, next_row)
    cw = cw_ref[...]
    y = gb_ref[...].astype(F32) * (ch_m1 * cw[0:1, :] + ch * cw[1:2, :] + ch_p1 * cw[2:3, :])
    merged_ref[:, aw:] = _rms(y, gc_ref[...]).astype(BF16)

    o_ref[...] = x_ref[...] + jnp.dot(merged_ref[...], wo_ref[...],
                                      preferred_element_type=F32)


def _outproj_call(attn, gb, ch, x2, wo, cw, ga, gc, *, seq, tm):
    n, d = x2.shape
    aw = attn.shape[1]
    cwid = gb.shape[1]
    tiles_per_seq = seq // tm
    blk8 = tm // SUBLANES
    last8 = n // SUBLANES - 1
    const = lambda i: (0, 0)
    kern = functools.partial(_outproj_kernel, tiles_per_seq=tiles_per_seq)
    return pl.pallas_call(
        kern,
        grid=(n // tm,),
        in_specs=[
            pl.BlockSpec((tm, aw), lambda i: (i, 0)),
            pl.BlockSpec((tm, cwid), lambda i: (i, 0)),
            pl.BlockSpec((tm, cwid), lambda i: (i, 0)),
            pl.BlockSpec((SUBLANES, cwid), lambda i: (jnp.maximum(i * blk8 - 1, 0), 0)),
            pl.BlockSpec((SUBLANES, cwid), lambda i: (jnp.minimum((i + 1) * blk8, last8), 0)),
            pl.BlockSpec((tm, d), lambda i: (i, 0)),
            pl.BlockSpec(wo.shape, const),
            pl.BlockSpec(cw.shape, const),
            pl.BlockSpec(ga.shape, const),
            pl.BlockSpec(gc.shape, const),
        ],
        out_specs=pl.BlockSpec((tm, d), lambda i: (i, 0)),
        out_shape=jax.ShapeDtypeStruct((n, d), F32),
        scratch_shapes=[pltpu.VMEM((tm, aw + cwid), BF16)],
        compiler_params=_compiler_params(("arbitrary",)),
        name="out_proj",
    )(attn, gb, ch, ch, ch, x2, wo, cw, ga, gc)


def _ffn_kernel(x_ref, xp_ref, xn_ref, g_ref, wg_ref, wv_ref, cw_ref, b_ref, wd_ref, gf_ref,
                o_ref, h_ref, a_ref, *, tiles_per_seq):
    i = pl.program_id(0)
    j = pl.program_id(1)
    tm = x_ref.shape[0]
    tn = wd_ref.shape[0]

    @pl.when(j == 0)
    def _():
        g = g_ref[...]
        x = x_ref[...]
        h_ref[0:tm, :] = _rms(x, g).astype(BF16)
        not_first = (i % tiles_per_seq != 0).astype(F32)
        not_last = (i % tiles_per_seq != tiles_per_seq - 1).astype(F32)
        halo = jnp.concatenate([_rms(xp_ref[...], g) * not_first,
                                _rms(xn_ref[...], g) * not_last], axis=0)
        h_ref[tm:tm + BF16_ROWS, :] = halo.astype(BF16)
        o_ref[...] = x

    h = h_ref[...]

    n_chains = tn // MXU_COLS
    for c in range(n_chains):
        grp = slice(c * MXU_COLS, (c + 1) * MXU_COLS)
        for half, w_ref in enumerate((wg_ref, wv_ref)):
            dst = slice(half * MXU_COLS, (half + 1) * MXU_COLS)
            a = jnp.dot(h, w_ref[:, grp], preferred_element_type=F32)
            a_ref[c, SUBLANES:SUBLANES + tm, dst] = a[0:tm]
            a_ref[c, 0:SUBLANES, dst] = a[tm:tm + SUBLANES]
            a_ref[c, SUBLANES + tm:2 * SUBLANES + tm, dst] = a[tm + SUBLANES:tm + 2 * SUBLANES]
    for c in range(n_chains):
        cols = slice(2 * c * MXU_COLS, 2 * (c + 1) * MXU_COLS)
        cw = cw_ref[:, cols]
        u = (a_ref[c, SUBLANES - 1:SUBLANES - 1 + tm, :] * cw[0:1, :]
             + a_ref[c, SUBLANES:SUBLANES + tm, :] * cw[1:2, :]
             + a_ref[c, SUBLANES + 1:SUBLANES + 1 + tm, :] * cw[2:3, :] + b_ref[:, cols])
        ug = u[:, :MXU_COLS]
        uv = u[:, MXU_COLS:]
        act = (ug * (1.0 / (1.0 + jnp.exp(-ug))) * uv).astype(BF16)
        o_ref[...] += jnp.dot(act, wd_ref[c * MXU_COLS:(c + 1) * MXU_COLS, :],
                              preferred_element_type=F32)

    @pl.when(j == pl.num_programs(1) - 1)
    def _():
        o_ref[...] = _rms(o_ref[...], gf_ref[...])


def _ffn_call(x1, g, wup, cw, b, wd, gf, *, seq, tm, tn):
    n, d = x1.shape
    dff = wd.shape[0]
    nj = dff // tn
    tiles_per_seq = seq // tm
    blk8 = tm // SUBLANES
    last8 = n // SUBLANES - 1
    const = lambda i, j: (0, 0)
    kern = functools.partial(_ffn_kernel, tiles_per_seq=tiles_per_seq)
    return pl.pallas_call(
        kern,
        grid=(n // tm, nj),
        in_specs=[
            pl.BlockSpec((tm, d), lambda i, j: (i, 0), pipeline_mode=pl.Buffered(1)),
            pl.BlockSpec((SUBLANES, d), lambda i, j: (jnp.maximum(i * blk8 - 1, 0), 0)),
            pl.BlockSpec((SUBLANES, d), lambda i, j: (jnp.minimum((i + 1) * blk8, last8), 0)),
            pl.BlockSpec(g.shape, const),
            pl.BlockSpec((d, tn), lambda i, j: (0, j)),
            pl.BlockSpec((d, tn), lambda i, j: (0, j + nj)),
            pl.BlockSpec((3, 2 * tn), lambda i, j: (0, j)),
            pl.BlockSpec((1, 2 * tn), lambda i, j: (0, j)),
            pl.BlockSpec((tn, d), lambda i, j: (j, 0)),
            pl.BlockSpec(gf.shape, const),
        ],
        out_specs=pl.BlockSpec((tm, d), lambda i, j: (i, 0)),
        out_shape=jax.ShapeDtypeStruct((n, d), F32),
        scratch_shapes=[pltpu.VMEM((tm + BF16_ROWS, d), BF16),
                        pltpu.VMEM((tn // MXU_COLS, tm + 2 * SUBLANES, 2 * MXU_COLS), F32)],
        compiler_params=_compiler_params(("arbitrary", "arbitrary")),
        name="conv_ffn",
    )(x1, x1, x1, g, wup, wup, cw, b, wd, gf)


def _rope_tables(seq):
    pos = jnp.arange(seq, dtype=F32)
    inv_freq = 1.0 / (ROPE_THETA ** (jnp.arange(0, QK_ROPE_DIM, 2, dtype=F32) / QK_ROPE_DIM))
    ang = pos[:, None] * inv_freq[None, :]
    c = jnp.cos(ang)
    s = jnp.sin(ang)
    z = jnp.zeros_like(c)
    return (jnp.concatenate([c, c, z, z], axis=-1),
            jnp.concatenate([-s, z, z, z], axis=-1),
            jnp.concatenate([z, s, z, z], axis=-1))


def _pair_gate_value_cols(w):
    r, two_dff = w.shape
    groups = two_dff // (2 * MXU_COLS)
    return w.reshape(r, 2, groups, MXU_COLS).transpose(0, 2, 1, 3).reshape(r, two_dff)


def _layer(x2, p, rope_t, *, batch, seq):
    d = x2.shape[1]
    q_rank = p["q_a_norm_g"].shape[0]
    kv_rank = p["kv_a_norm_g"].shape[0]
    lat = q_rank + kv_rank
    w_in = p["w_in"].astype(BF16)
    wbch = w_in[:, lat + QK_ROPE_DIM:]
    wq = p["w_q_b"].reshape(q_rank, MLA_HEADS, QK_HEAD_DIM)
    wqb = jnp.pad(wq, ((0, 0), (0, 0), (0, QK_PAD_DIM - QK_HEAD_DIM))).reshape(
        q_rank, MLA_HEADS * QK_PAD_DIM).astype(BF16)
    wkvb = p["w_kv_b"].astype(BF16)
    row = lambda v: v.reshape(1, -1)

    q, k, v = _latent_call(x2, row(p["attn_norm_g"]), w_in, row(p["q_a_norm_g"]),
                           row(p["kv_a_norm_g"]), wqb, wkvb, rope_t,
                           batch=batch, seq=seq, tm=512)
    gb, ch = _gate_call(x2, row(p["attn_norm_g"]), wbch, tm=512)
    attn = _attn_call(q, k, v, tq=512, tk=1024, q_group=4).reshape(batch * seq, -1)
    x1 = _outproj_call(attn, gb, ch, x2, p["w_o"].astype(BF16), p["sc_conv_w"],
                       row(p["out_norm_attn_g"]), row(p["out_norm_conv_g"]), seq=seq, tm=512)
    return x1


def kernel(x, attn_norm_g, w_in, q_a_norm_g, kv_a_norm_g, w_q_b, w_kv_b, sc_conv_w,
           out_norm_attn_g, out_norm_conv_g, w_o, ffn_norm_g, w_ffn_up, ffn_conv_w,
           ffn_conv_b, w_ffn_down, final_norm_g):
    batch, seq, d = x.shape
    depth = w_in.shape[0]
    assert depth == 1, "the final RMSNorm is fused into the (single) ConvFFN layer"
    rope_t = _rope_tables(seq)
    x2 = x.reshape(batch * seq, d)
    p = dict(attn_norm_g=attn_norm_g[0], w_in=w_in[0], q_a_norm_g=q_a_norm_g[0],
             kv_a_norm_g=kv_a_norm_g[0], w_q_b=w_q_b[0], w_kv_b=w_kv_b[0],
             sc_conv_w=sc_conv_w[0], out_norm_attn_g=out_norm_attn_g[0],
             out_norm_conv_g=out_norm_conv_g[0], w_o=w_o[0])
    x1 = _layer(x2, p, rope_t, batch=batch, seq=seq)
    out = _ffn_call(x1, ffn_norm_g[0].reshape(1, -1),
                    w_ffn_up[0].astype(BF16),
                    _pair_gate_value_cols(ffn_conv_w[0]),
                    _pair_gate_value_cols(ffn_conv_b[0].reshape(1, -1)),
                    w_ffn_down[0].astype(BF16),
                    final_norm_g.reshape(1, -1), seq=seq, tm=1024, tn=512)
    return out.reshape(batch, seq, d)
```
